```python
import jax, jax.numpy as jnp
from jax import lax
import numpy as np

D_MODEL = 1024
BATCH = 16
SEQ = 256
DEPTH = 1
DEC_BATCH = 4
DEC_SEQ = 2048
PAST_LEN = 256

GRID_W = 64
SSD_EXPAND = 2
SSD_D_INNER = SSD_EXPAND * D_MODEL
SSD_HEADDIM = 64
SSD_HEADS = SSD_D_INNER // SSD_HEADDIM
SSD_GROUPS = 8
D_STATE = 128
CONV_W = 5
CHUNK = 128
XBC_DIM = SSD_D_INNER + 2 * SSD_GROUPS * D_STATE
N_HEADS = 16
N_KV_HEADS = 4
HEAD_DIM = D_MODEL // N_HEADS
Q_BLOCK = 128
ROPE_THETA = 10000.0
N_KEYS = 128
N_EXPERTS = N_KEYS * N_KEYS
PEER_HEADS = 8
PEER_TOPK = 16
PEER_DQ = 256
TOK_BLOCK = 128
EPS = 1e-6

IN_SIZES = (SSD_D_INNER, XBC_DIM, SSD_HEADS, SSD_HEADS, N_HEADS * HEAD_DIM,
            N_KV_HEADS * HEAD_DIM, N_KV_HEADS * HEAD_DIM, D_MODEL, D_MODEL)
IN_DIM = SSD_D_INNER + XBC_DIM + 2 * SSD_HEADS + N_HEADS * HEAD_DIM + 2 * N_KV_HEADS * HEAD_DIM + 2 * D_MODEL

kernel_name = "hybrid_ssd_gqa_peer_diffusion_step"


def rmsnorm(x, w):
    xf = x.astype(jnp.float32)
    y = xf * lax.rsqrt(jnp.mean(xf * xf, axis=-1, keepdims=True) + EPS)
    return (y * w.astype(jnp.float32)).astype(x.dtype)


def modulate(x, shift, scale):
    return x * (1 + scale) + shift


def depthwise_conv(x, w, b):
    pad = CONV_W // 2
    y = lax.conv_general_dilated(x, w.astype(x.dtype)[:, None, :], window_strides=(1,),
                                 padding=[(pad, pad)], dimension_numbers=('NWC', 'WIO', 'NWC'),
                                 feature_group_count=x.shape[-1])
    return y + b


def ssd_scan(x, dt, a_log, b_in, c_in, init_state):
    f32 = jnp.float32
    dtype = x.dtype
    bsz, t, h, p = x.shape
    nc = t // CHUNK
    rep = h // b_in.shape[2]
    A = -jnp.exp(a_log.astype(f32))
    Bh = jnp.repeat(b_in.astype(f32), rep, axis=2).reshape(bsz, nc, CHUNK, h, -1)
    Ch = jnp.repeat(c_in.astype(f32), rep, axis=2).reshape(bsz, nc, CHUNK, h, -1)
    dtf = dt.astype(f32)
    xdt = (x.astype(f32) * dtf[..., None]).reshape(bsz, nc, CHUNK, h, p)
    a_cs = jnp.cumsum((dtf * A).reshape(bsz, nc, CHUNK, h), axis=2)
    seg = a_cs[:, :, :, None, :] - a_cs[:, :, None, :, :]
    lower = jnp.tril(jnp.ones((CHUNK, CHUNK), dtype=bool))[None, None, :, :, None]
    decay_in = jnp.exp(jnp.where(lower, seg, -jnp.inf))
    scores = jnp.einsum('bclhn,bcshn->bclsh', Ch, Bh)
    y_diag = jnp.einsum('bclsh,bcshp->bclhp', scores * decay_in, xdt)
    decay_to_end = jnp.exp(a_cs[:, :, -1:, :] - a_cs)
    chunk_states = jnp.einsum('bclhn,bclh,bclhp->bchpn', Bh, decay_to_end, xdt)
    chunk_decay = jnp.exp(a_cs[:, :, -1, :])

    def step(s, inp):
        st, dc = inp
        return s * dc[:, :, None, None] + st, s

    final, prev = lax.scan(step, init_state.astype(f32),
                           (jnp.moveaxis(chunk_states, 1, 0), jnp.moveaxis(chunk_decay, 1, 0)))
    prev = jnp.moveaxis(prev, 0, 1)
    y_off = jnp.einsum('bclhn,bchpn,bclh->bclhp', Ch, prev, jnp.exp(a_cs))
    y = (y_diag + y_off).reshape(bsz, t, h, p)
    return y.astype(dtype), final.astype(dtype)


def rope_1d(v, pos):
    half = v.shape[-1] // 2
    inv = ROPE_THETA ** (-jnp.arange(half, dtype=jnp.float32) / half)
    ang = pos.astype(jnp.float32)[:, None] * inv
    cos = jnp.cos(ang)[None, :, None, :]
    sin = jnp.sin(ang)[None, :, None, :]
    vf = v.astype(jnp.float32)
    v1, v2 = vf[..., :half], vf[..., half:]
    return jnp.concatenate([v1 * cos - v2 * sin, v1 * sin + v2 * cos], axis=-1).astype(v.dtype)


def axial_rope(v, row_pos, col_pos):
    hd2 = v.shape[-1] // 2
    return jnp.concatenate([rope_1d(v[..., :hd2], row_pos), rope_1d(v[..., hd2:], col_pos)], axis=-1)


def attend(q, k, v):
    bsz, tq = q.shape[:2]
    nq = tq // Q_BLOCK
    g = N_HEADS // N_KV_HEADS
    qb = q.reshape(bsz, nq, Q_BLOCK, N_KV_HEADS, g, HEAD_DIM).transpose(1, 0, 2, 3, 4, 5)
    scale = HEAD_DIM ** -0.5

    def blk(qi):
        s = jnp.einsum('bqkgd,bskd->bkgqs', qi, k).astype(jnp.float32) * scale
        p = jax.nn.softmax(s, axis=-1).astype(v.dtype)
        return jnp.einsum('bkgqs,bskd->bqkgd', p, v)

    o = lax.map(blk, qb)
    return o.transpose(1, 0, 2, 3, 4, 5).reshape(bsz, tq, N_HEADS * HEAD_DIM)


def peer(x, w_pq, sub_keys, u_tab, v_tab):
    bsz, t, d = x.shape
    xb = x.reshape(-1, TOK_BLOCK, d)

    def blk(xi):
        q = (xi @ w_pq).reshape(TOK_BLOCK, PEER_HEADS, 2, PEER_DQ // 2)
        s = jnp.einsum('thcd,hckd->thck', q, sub_keys).astype(jnp.float32)
        top_s, top_i = lax.top_k(s, PEER_TOPK)
        cand_s = (top_s[:, :, 0, :, None] + top_s[:, :, 1, None, :]).reshape(TOK_BLOCK, PEER_HEADS, -1)
        cand_i = (top_i[:, :, 0, :, None] * N_KEYS + top_i[:, :, 1, None, :]).reshape(TOK_BLOCK, PEER_HEADS, -1)
        best_s, pos = lax.top_k(cand_s, PEER_TOPK)
        idx = jnp.take_along_axis(cand_i, pos, axis=-1)
        gate = jax.nn.softmax(best_s, axis=-1).astype(xi.dtype)
        act = jax.nn.gelu(jnp.einsum('td,thkd->thk', xi, u_tab[idx]))
        return jnp.einsum('thk,thkd->td', gate * act, v_tab[idx])

    return lax.map(blk, xb).reshape(bsz, t, d)


def token_mixer(h, init_f, init_b, ctx_k, ctx_v, pos, lp):
    bsz, t, _ = h.shape
    proj = h @ lp['w_in']
    idx, acc = [], 0
    for s in IN_SIZES[:-1]:
        acc += s
        idx.append(acc)
    z, xbc, dt_f, dt_b, q, k, v, g_a, g_b = jnp.split(proj, idx, axis=-1)

    xbc = jax.nn.silu(depthwise_conv(xbc, lp['conv_w'], lp['conv_b']))
    xs, bm, cm = jnp.split(xbc, [SSD_D_INNER, SSD_D_INNER + SSD_GROUPS * D_STATE], axis=-1)
    xs = xs.reshape(bsz, t, SSD_HEADS, SSD_HEADDIM)
    bm = bm.reshape(bsz, t, SSD_GROUPS, D_STATE)
    cm = cm.reshape(bsz, t, SSD_GROUPS, D_STATE)
    dt_f = jax.nn.softplus(dt_f + lp['dt_bias_f'])
    dt_b = jax.nn.softplus(dt_b + lp['dt_bias_b'])
    if init_f is None:
        init_f = jnp.zeros((bsz, SSD_HEADS, SSD_HEADDIM, D_STATE), h.dtype)
        init_b = jnp.zeros((bsz, SSD_HEADS, SSD_HEADDIM, D_STATE), h.dtype)
    y_f, s_f = ssd_scan(xs, dt_f, lp['a_log_f'], bm, cm, init_f)
    fl = lambda a: jnp.flip(a, axis=1)
    y_b, s_b = ssd_scan(fl(xs), fl(dt_b), lp['a_log_b'], fl(bm), fl(cm), init_b)
    y = y_f + fl(y_b) + lp['d_skip'][:, None] * xs
    y = rmsnorm(y.reshape(bsz, t, SSD_D_INNER) * jax.nn.silu(z), lp['ssd_norm_w'])
    y_a = y @ lp['w_ssd_out']

    q = rmsnorm(q.reshape(bsz, t, N_HEADS, HEAD_DIM), lp['q_norm_w'])
    k = rmsnorm(k.reshape(bsz, t, N_KV_HEADS, HEAD_DIM), lp['k_norm_w'])
    v = v.reshape(bsz, t, N_KV_HEADS, HEAD_DIM)
    if ctx_k is None:
        keys, vals = k, v
    else:
        q = axial_rope(q, pos[0], pos[1])
        keys = jnp.concatenate([axial_rope(k, pos[0], pos[1]), ctx_k], axis=1)
        vals = jnp.concatenate([v, ctx_v], axis=1)
    y_b2 = attend(q, keys, vals) @ lp['w_attn_out']

    merged = jax.nn.sigmoid(g_a) * y_a + jax.nn.sigmoid(g_b) * y_b2
    return merged @ lp['w_o'], s_f, s_b, k, v


def layer(x, mod, init_f, init_b, ctx_k, ctx_v, pos, lp):
    shift1, scale1, gate1, shift2, scale2, gate2 = jnp.split(mod, 6, axis=-1)
    h = modulate(rmsnorm(x, lp['norm1_w']), shift1, scale1)
    mix, s_f, s_b, k, v = token_mixer(h, init_f, init_b, ctx_k, ctx_v, pos, lp)
    x = x + gate1 * mix
    h = modulate(rmsnorm(x, lp['norm2_w']), shift2, scale2)
    x = x + gate2 * peer(h, lp['w_pq'], lp['peer_keys'], lp['peer_u'], lp['peer_v'])
    return x, s_f, s_b, k, v


def setup_inputs(seed: int = 0) -> dict:
    key = jax.random.key(seed)
    ks = jax.random.split(key, 40)
    f32 = jnp.float32
    nrm = lambda i, shape, s: jax.random.normal(ks[i], shape, f32) * s
    dt0 = jnp.exp(jax.random.uniform(ks[30], (DEPTH, SSD_HEADS), f32, np.log(1e-3), np.log(1e-1)))
    dt1 = jnp.exp(jax.random.uniform(ks[31], (DEPTH, SSD_HEADS), f32, np.log(1e-3), np.log(1e-1)))
    inv_sp = lambda d: d + jnp.log(-jnp.expm1(-d))
    return {
        "x_prompt": nrm(0, (BATCH, SEQ, D_MODEL), 1.0),
        "x_sample": nrm(1, (DEC_BATCH, DEC_SEQ, D_MODEL), 1.0),
        "cache_k": nrm(2, (DEC_BATCH, DEPTH, PAST_LEN, N_KV_HEADS, HEAD_DIM), 1.0),
        "cache_v": nrm(3, (DEC_BATCH, DEPTH, PAST_LEN, N_KV_HEADS, HEAD_DIM), 1.0),
        "state_ssm_fwd": nrm(4, (DEC_BATCH, DEPTH, SSD_HEADS, SSD_HEADDIM, D_STATE), 0.5),
        "state_ssm_bwd": nrm(5, (DEC_BATCH, DEPTH, SSD_HEADS, SSD_HEADDIM, D_STATE), 0.5),
        "c": nrm(6, (DEC_BATCH, D_MODEL), 1.0),
        "c_ctx": nrm(7, (D_MODEL,), 1.0),
        "w_ada": nrm(8, (DEPTH, D_MODEL, 6 * D_MODEL), 0.5 * D_MODEL ** -0.5),
        "b_ada": nrm(9, (DEPTH, 6 * D_MODEL), 0.01),
        "norm1_w": 1.0 + nrm(10, (DEPTH, D_MODEL), 0.02),
        "norm2_w": 1.0 + nrm(11, (DEPTH, D_MODEL), 0.02),
        "w_in": nrm(12, (DEPTH, D_MODEL, IN_DIM), D_MODEL ** -0.5),
        "conv_w": nrm(13, (DEPTH, CONV_W, XBC_DIM), CONV_W ** -0.5),
        "conv_b": nrm(14, (DEPTH, XBC_DIM), 0.01),
        "a_log_f": jnp.log(jax.random.uniform(ks[15], (DEPTH, SSD_HEADS), f32, 1.0, 16.0)),
        "a_log_b": jnp.log(jax.random.uniform(ks[16], (DEPTH, SSD_HEADS), f32, 1.0, 16.0)),
        "dt_bias_f": inv_sp(dt0),
        "dt_bias_b": inv_sp(dt1),
        "d_skip": 1.0 + nrm(17, (DEPTH, SSD_HEADS), 0.1),
        "ssd_norm_w": 1.0 + nrm(18, (DEPTH, SSD_D_INNER), 0.02),
        "w_ssd_out": nrm(19, (DEPTH, SSD_D_INNER, D_MODEL), SSD_D_INNER ** -0.5),
        "q_norm_w": 1.0 + nrm(20, (DEPTH, HEAD_DIM), 0.02),
        "k_norm_w": 1.0 + nrm(21, (DEPTH, HEAD_DIM), 0.02),
        "w_attn_out": nrm(22, (DEPTH, N_HEADS * HEAD_DIM, D_MODEL), (N_HEADS * HEAD_DIM) ** -0.5),
        "w_o": nrm(23, (DEPTH, D_MODEL, D_MODEL), D_MODEL ** -0.5),
        "w_pq": nrm(24, (DEPTH, D_MODEL, PEER_HEADS * PEER_DQ), D_MODEL ** -0.5),
        "peer_keys": nrm(25, (DEPTH, PEER_HEADS, 2, N_KEYS, PEER_DQ // 2), (PEER_DQ // 2) ** -0.5),
        "peer_u": nrm(26, (DEPTH, N_EXPERTS, D_MODEL), D_MODEL ** -0.5),
        "peer_v": nrm(27, (DEPTH, N_EXPERTS, D_MODEL), 0.5),
        "final_norm_w": 1.0 + nrm(28, (D_MODEL,), 0.02),
    }


def reference(x_prompt, x_sample, cache_k, cache_v, state_ssm_fwd, state_ssm_bwd, c, c_ctx,
              w_ada, b_ada, norm1_w, norm2_w, w_in, conv_w, conv_b, a_log_f, a_log_b,
              dt_bias_f, dt_bias_b, d_skip, ssd_norm_w, w_ssd_out, q_norm_w, k_norm_w,
              w_attn_out, w_o, w_pq, peer_keys, peer_u, peer_v, final_norm_w):
    n_lat = x_sample.shape[1]
    rows = n_lat // GRID_W
    row_pos = jnp.repeat(jnp.arange(rows), GRID_W)
    col_pos = jnp.tile(jnp.arange(GRID_W), rows)
    pos = (row_pos, col_pos)

    xp, xs = x_prompt, x_sample
    new_k, new_v, new_sf, new_sb = [], [], [], []
    for l in range(DEPTH):
        lp = {
            'norm1_w': norm1_w[l], 'norm2_w': norm2_w[l], 'w_in': w_in[l],
            'conv_w': conv_w[l], 'conv_b': conv_b[l], 'a_log_f': a_log_f[l], 'a_log_b': a_log_b[l],
            'dt_bias_f': dt_bias_f[l], 'dt_bias_b': dt_bias_b[l], 'd_skip': d_skip[l],
            'ssd_norm_w': ssd_norm_w[l], 'w_ssd_out': w_ssd_out[l], 'q_norm_w': q_norm_w[l],
            'k_norm_w': k_norm_w[l], 'w_attn_out': w_attn_out[l], 'w_o': w_o[l],
            'w_pq': w_pq[l], 'peer_keys': peer_keys[l], 'peer_u': peer_u[l], 'peer_v': peer_v[l],
        }
        mod_ctx = jax.nn.silu(c_ctx) @ w_ada[l] + b_ada[l]
        mod_lat = (jax.nn.silu(c) @ w_ada[l] + b_ada[l])[:, None, :]
        xp, sf, sb, kc, vc = layer(xp, mod_ctx, None, None, None, None, None, lp)
        new_k.append(kc); new_v.append(vc); new_sf.append(sf); new_sb.append(sb)
        xs, _, _, _, _ = layer(xs, mod_lat, state_ssm_fwd[:, l], state_ssm_bwd[:, l],
                               cache_k[:, l], cache_v[:, l], pos, lp)
    y_prompt = rmsnorm(xp, final_norm_w)
    y_sample = rmsnorm(xs, final_norm_w)
    new_cache_k = jnp.stack(new_k, axis=1)
    new_cache_v = jnp.stack(new_v, axis=1)
    new_state_fwd = jnp.stack(new_sf, axis=1)
    new_state_bwd = jnp.stack(new_sb, axis=1)
    return (y_prompt, y_sample, new_cache_k, new_cache_v, new_state_fwd, new_state_bwd)
```

```python
import functools

import jax
import jax.numpy as jnp
import numpy as np
from jax import lax
from jax.experimental import pallas as pl
from jax.experimental.pallas import tpu as pltpu

F32 = jnp.float32
BF16 = jnp.bfloat16

D_MODEL = 1024
GRID_W = 64
SSD_D_INNER = 2048
SSD_HEADDIM = 64
SSD_HEADS = 32
SSD_GROUPS = 8
D_STATE = 128
CONV_W = 5
CHUNK = 128
XBC_DIM = 4096
N_HEADS = 16
N_KV_HEADS = 4
HEAD_DIM = 64
KV_DIM = N_KV_HEADS * HEAD_DIM
ROPE_THETA = 10000.0
N_KEYS = 128
PEER_HEADS = 8
PEER_TOPK = 16
PEER_DQ = 256
EPS = 1e-6

LANES = 128
MIB = 1024 * 1024

IN_TN = 1024
COL_Z = 0
COL_XBC = 2
COL_Q = 6
COL_GA = 7
COL_GB = 8
N_MAIN = 9
MAIN_W = N_MAIN * IN_TN
SMALL_W = IN_TN

NT_DIMS = (((1,), (1,)), ((), ()))


def _cparams(sem, vmem_mib):
    return pltpu.CompilerParams(dimension_semantics=sem, vmem_limit_bytes=vmem_mib * MIB)


def _sigmoid(x):
    return 1.0 / (1.0 + jnp.exp(-x))


def _silu(x):
    return x * _sigmoid(x)


def _softplus(x):
    return jnp.maximum(x, 0.0) + jnp.log(1.0 + jnp.exp(-jnp.abs(x)))


def _ada_kernel(c_ref, w_ref, b_ref, o_ref):
    sc = _silu(c_ref[...])
    o_ref[...] = jnp.dot(sc.astype(BF16), w_ref[...].astype(BF16),
                         preferred_element_type=F32) + b_ref[...]


def _ada(cc, w, b):
    rows, d = cc.shape
    n = w.shape[1]
    tn = 1536
    return pl.pallas_call(
        _ada_kernel,
        grid=(n // tn,),
        in_specs=[pl.BlockSpec((rows, d), lambda j: (0, 0)),
                  pl.BlockSpec((d, tn), lambda j: (0, j)),
                  pl.BlockSpec((1, tn), lambda j: (0, j))],
        out_specs=pl.BlockSpec((rows, tn), lambda j: (0, j)),
        out_shape=jax.ShapeDtypeStruct((rows, n), F32),
        compiler_params=_cparams(("arbitrary",), 32),
        name="ada",
    )(cc, w, b)


def _inproj_kernel(x_ref, mod_ref, nw_ref, w_ref, main_ref, small_ref, h_ref):
    j = pl.program_id(2)

    @pl.when(j == 0)
    def _():
        x = x_ref[...]
        r = lax.rsqrt(jnp.mean(x * x, axis=-1, keepdims=True) + EPS)
        shift = mod_ref[:, 0:D_MODEL]
        scale = mod_ref[:, D_MODEL:2 * D_MODEL]
        h_ref[...] = ((x * r * nw_ref[...]) * (1.0 + scale) + shift).astype(BF16)

    acc = jnp.dot(h_ref[...], w_ref[...], preferred_element_type=F32)

    @pl.when(j < N_MAIN)
    def _():
        main_ref[...] = acc.astype(BF16)

    @pl.when(j == N_MAIN)
    def _():
        small_ref[...] = acc


def _inproj(x, mod, nw, w_cat, tm):
    bm, t, d = x.shape
    return pl.pallas_call(
        _inproj_kernel,
        grid=(bm, t // tm, N_MAIN + 1),
        in_specs=[pl.BlockSpec((None, tm, d), lambda b, i, j: (b, i, 0)),
                  pl.BlockSpec((None, 1, 6 * d), lambda b, i, j: (b, 0, 0)),
                  pl.BlockSpec((1, d), lambda b, i, j: (0, 0)),
                  pl.BlockSpec((d, IN_TN), lambda b, i, j: (0, j))],
        out_specs=[pl.BlockSpec((None, tm, IN_TN), lambda b, i, j: (b, i, jnp.minimum(j, N_MAIN - 1))),
                   pl.BlockSpec((None, tm, SMALL_W), lambda b, i, j: (b, i, 0))],
        out_shape=[jax.ShapeDtypeStruct((bm, t, MAIN_W), BF16),
                   jax.ShapeDtypeStruct((bm, t, SMALL_W), F32)],
        scratch_shapes=[pltpu.VMEM((tm, d), BF16)],
        compiler_params=_cparams(("arbitrary", "arbitrary", "arbitrary"), 40),
        name="inproj",
    )(x, mod, nw, w_cat)


CONV_HALO = 16


def _conv_kernel(cur_ref, prev_ref, next_ref, w_ref, b_ref, o_ref, *, nt, tt):
    i = pl.program_id(1)
    cur = cur_ref[...].astype(F32)
    pv = prev_ref[...].astype(F32)[8:16]
    nx = next_ref[...].astype(F32)[0:8]
    pv = jnp.where(i > 0, pv, 0.0)
    nx = jnp.where(i < nt - 1, nx, 0.0)
    ext = jnp.concatenate([pv, cur, nx], axis=0)
    acc = jnp.broadcast_to(b_ref[...], cur.shape)
    for k in range(CONV_W):
        off = 8 + k - CONV_W // 2
        acc = acc + w_ref[k:k + 1, :] * ext[off:off + tt]
    o_ref[...] = _silu(acc).astype(BF16)


def _conv(main, conv_w, conv_b, tt):
    b, t, _ = main.shape
    tc = IN_TN
    nt = t // tt
    hb = tt // CONV_HALO
    last = t // CONV_HALO - 1
    return pl.pallas_call(
        functools.partial(_conv_kernel, nt=nt, tt=tt),
        grid=(b, nt, XBC_DIM // tc),
        in_specs=[pl.BlockSpec((None, tt, tc), lambda bb, i, j: (bb, i, COL_XBC + j)),
                  pl.BlockSpec((None, CONV_HALO, tc),
                               lambda bb, i, j: (bb, jnp.maximum(i * hb - 1, 0), COL_XBC + j)),
                  pl.BlockSpec((None, CONV_HALO, tc),
                               lambda bb, i, j: (bb, jnp.minimum((i + 1) * hb, last), COL_XBC + j)),
                  pl.BlockSpec((8, tc), lambda bb, i, j: (0, j)),
                  pl.BlockSpec((1, tc), lambda bb, i, j: (0, j))],
        out_specs=pl.BlockSpec((None, tt, tc), lambda bb, i, j: (bb, i, j)),
        out_shape=jax.ShapeDtypeStruct((b, t, XBC_DIM), BF16),
        compiler_params=_cparams(("arbitrary", "arbitrary", "arbitrary"), 32),
        name="conv",
    )(main, main, main, conv_w, conv_b)


N_PAIRS = SSD_HEADS // 2


def _ssd_kernel(*refs, nc, has_init):
    if has_init:
        xs_ref, b_ref, c_ref, dt_ref, prm_ref, init_ref, y_ref, fin_ref, st_ref = refs
    else:
        xs_ref, b_ref, c_ref, dt_ref, prm_ref, y_ref, fin_ref, st_ref = refs
    d = pl.program_id(0)
    c = pl.program_id(2)

    @pl.when(c == 0)
    def _():
        for pp in range(N_PAIRS):
            if has_init:
                st_ref[pp] = init_ref[pp * LANES:(pp + 1) * LANES, :].T
            else:
                st_ref[pp] = jnp.zeros((D_STATE, LANES), F32)

    row = lax.broadcasted_iota(jnp.int32, (CHUNK, CHUNK), 0)
    col = lax.broadcasted_iota(jnp.int32, (CHUNK, CHUNK), 1)
    tri = (row - col) * (1 - 2 * d) >= 0
    lo = col < SSD_HEADDIM

    a_log = prm_ref[0:1, :]
    bias = prm_ref[1:2, :]
    dt = _softplus(dt_ref[...] + bias)
    a = dt * (-jnp.exp(a_log))
    cs = jnp.dot(tri.astype(F32), a, precision=lax.Precision.HIGHEST,
                 preferred_element_type=F32)
    tot = jnp.sum(a, axis=0, keepdims=True)
    w_t = (jnp.exp(tot - cs) * dt).T
    r_t = (cs - jnp.log(dt)).T
    dec = jnp.exp(tot)

    for g in range(SSD_GROUPS):
        bg = b_ref[:, g * D_STATE:(g + 1) * D_STATE]
        cg = c_ref[:, g * D_STATE:(g + 1) * D_STATE]
        sc = lax.dot_general(cg, bg, NT_DIMS, preferred_element_type=F32)
        bg_t = bg.astype(F32).T
        cgf = cg.astype(F32)
        for q in range(2):
            pp = g * 2 + q
            xp = xs_ref[:, pp * LANES:(pp + 1) * LANES]
            st_old = st_ref[pp]
            rhs = jnp.concatenate([xp, st_old.astype(BF16)], axis=0)
            ys, us = [], []
            for e in range(2):
                h = pp * 2 + e
                csb = jnp.broadcast_to(cs[:, h:h + 1], (CHUNK, CHUNK))
                m = sc * jnp.exp(jnp.where(tri, csb - r_t[h:h + 1, :], -jnp.inf))
                cse = cgf * jnp.exp(csb)
                lhs = jnp.concatenate([m.astype(BF16), cse.astype(BF16)], axis=1)
                ys.append(jnp.dot(lhs, rhs, preferred_element_type=F32))
                bw = (bg_t * w_t[h:h + 1, :]).astype(BF16)
                us.append(jnp.dot(bw, xp, preferred_element_type=F32))
            y_ref[:, pp * LANES:(pp + 1) * LANES] = jnp.where(lo, ys[0], ys[1]).astype(BF16)
            decrow = jnp.where(lo[0:1, :], dec[:, 2 * pp:2 * pp + 1], dec[:, 2 * pp + 1:2 * pp + 2])
            st_ref[pp] = st_old * decrow + jnp.where(lo, us[0], us[1])

    @pl.when(c == nc - 1)
    def _():
        for pp in range(N_PAIRS):
            fin_ref[pp * LANES:(pp + 1) * LANES, :] = st_ref[pp].T


def _ssd(xbc, small, prm, init):
    b, t, _ = xbc.shape
    nc = t // CHUNK
    has_init = init is not None

    def ce(d, c):
        return c + d * (nc - 1 - 2 * c)

    in_specs = [pl.BlockSpec((None, CHUNK, SSD_D_INNER), lambda d, bb, c: (bb, ce(d, c), 0)),
                pl.BlockSpec((None, CHUNK, SSD_GROUPS * D_STATE), lambda d, bb, c: (bb, ce(d, c), 2)),
                pl.BlockSpec((None, CHUNK, SSD_GROUPS * D_STATE), lambda d, bb, c: (bb, ce(d, c), 3)),
                pl.BlockSpec((None, CHUNK, LANES), lambda d, bb, c: (bb, ce(d, c), 4 + d)),
                pl.BlockSpec((None, 8, LANES), lambda d, bb, c: (d, 0, 0))]
    args = [xbc, xbc, xbc, small, prm]
    if has_init:
        in_specs.append(pl.BlockSpec((None, None, SSD_D_INNER, D_STATE), lambda d, bb, c: (d, bb, 0, 0)))
        args.append(init)
    return pl.pallas_call(
        functools.partial(_ssd_kernel, nc=nc, has_init=has_init),
        grid=(2, b, nc),
        in_specs=in_specs,
        out_specs=[pl.BlockSpec((None, None, CHUNK, SSD_D_INNER), lambda d, bb, c: (d, bb, ce(d, c), 0)),
                   pl.BlockSpec((None, None, SSD_D_INNER, D_STATE), lambda d, bb, c: (d, bb, 0, 0))],
        out_shape=[jax.ShapeDtypeStruct((2, b, t, SSD_D_INNER), BF16),
                   jax.ShapeDtypeStruct((2, b, SSD_D_INNER, D_STATE), F32)],
        scratch_shapes=[pltpu.VMEM((N_PAIRS, D_STATE, LANES), F32)],
        compiler_params=_cparams(("arbitrary", "arbitrary", "arbitrary"), 40),
        name="ssd",
    )(*args)


def _rope(x, cos, sin_a, sin_b, reps):
    n = x.shape[1]
    half = HEAD_DIM // 4
    cosf = jnp.concatenate([cos] * reps, axis=1)
    saf = jnp.concatenate([sin_a] * reps, axis=1)
    sbf = jnp.concatenate([sin_b] * reps, axis=1)
    return x * cosf + pltpu.roll(x, n - half, 1) * saf + pltpu.roll(x, half, 1) * sbf


def _prep_kernel(*refs, rope):
    if rope:
        q_ref, kv_ref, qnw_ref, knw_ref, bq_ref, bk_ref, cos_ref, sa_ref, sb_ref, qo_ref, ko_ref, vo_ref = refs
    else:
        q_ref, kv_ref, qnw_ref, knw_ref, bq_ref, bk_ref, qo_ref, ko_ref, vo_ref = refs
    q = q_ref[...].astype(F32)
    k = kv_ref[:, 0:KV_DIM]
    vo_ref[...] = kv_ref[:, KV_DIM:2 * KV_DIM]
    qss = jnp.dot((q * q).astype(BF16), bq_ref[...], preferred_element_type=F32)
    kss = jnp.dot((k * k).astype(BF16), bk_ref[...], preferred_element_type=F32)
    qn = q * lax.rsqrt(qss * (1.0 / HEAD_DIM) + EPS) * qnw_ref[...]
    kn = k * lax.rsqrt(kss * (1.0 / HEAD_DIM) + EPS) * knw_ref[...]
    if rope:
        cos, sa, sb = cos_ref[...], sa_ref[...], sb_ref[...]
        qn = _rope(qn, cos, sa, sb, D_MODEL // LANES)
        kn = _rope(kn, cos, sa, sb, KV_DIM // LANES)
    qo_ref[...] = (qn * (HEAD_DIM ** -0.5)).astype(BF16)
    ko_ref[...] = kn


def _prep(main, small, qnw_t, knw_t, bq, bk, tables, tt):
    b, t, _ = main.shape
    rope = tables is not None
    in_specs = [pl.BlockSpec((None, tt, D_MODEL), lambda bb, i: (bb, i, COL_Q)),
                pl.BlockSpec((None, tt, 2 * KV_DIM), lambda bb, i: (bb, i, 0)),
                pl.BlockSpec((1, D_MODEL), lambda bb, i: (0, 0)),
                pl.BlockSpec((1, KV_DIM), lambda bb, i: (0, 0)),
                pl.BlockSpec((D_MODEL, D_MODEL), lambda bb, i: (0, 0)),
                pl.BlockSpec((KV_DIM, KV_DIM), lambda bb, i: (0, 0))]
    args = [main, small, qnw_t, knw_t, bq, bk]
    if rope:
        in_specs += [pl.BlockSpec((tt, LANES), lambda bb, i: (i, 0))] * 3
        args += list(tables)
    return pl.pallas_call(
        functools.partial(_prep_kernel, rope=rope),
        grid=(b, t // tt),
        in_specs=in_specs,
        out_specs=[pl.BlockSpec((None, tt, D_MODEL), lambda bb, i: (bb, i, 0)),
                   pl.BlockSpec((None, tt, KV_DIM), lambda bb, i: (bb, i, 0)),
                   pl.BlockSpec((None, tt, KV_DIM), lambda bb, i: (bb, i, 0))],
        out_shape=[jax.ShapeDtypeStruct((b, t, D_MODEL), BF16),
                   jax.ShapeDtypeStruct((b, t, KV_DIM), F32),
                   jax.ShapeDtypeStruct((b, t, KV_DIM), F32)],
        compiler_params=_cparams(("arbitrary", "arbitrary"), 32),
        name="qk_prep",
    )(*args)


def _attn_kernel(q_ref, k_ref, v_ref, o_ref, kx_ref, vx_ref):
    qi = pl.program_id(1)
    tk = k_ref.shape[0]

    @pl.when(qi == 0)
    def _():
        lo = lax.broadcasted_iota(jnp.int32, (tk, LANES), 1) < HEAD_DIM
        for pr in range(KV_DIM // LANES):
            for src, dst in ((k_ref, kx_ref), (v_ref, vx_ref)):
                a = src[:, pr * LANES:(pr + 1) * LANES].astype(F32)
                ar = pltpu.roll(a, HEAD_DIM, 1)
                dst[4 * pr + 0] = jnp.where(lo, a, 0.0).astype(BF16)
                dst[4 * pr + 1] = jnp.where(lo, 0.0, ar).astype(BF16)
                dst[4 * pr + 2] = jnp.where(lo, ar, 0.0).astype(BF16)
                dst[4 * pr + 3] = jnp.where(lo, 0.0, a).astype(BF16)

    for pp in range(N_HEADS // 2):
        g = pp // 2
        qp = q_ref[:, pp * LANES:(pp + 1) * LANES]
        o = None
        for e in range(2):
            s = lax.dot_general(qp, kx_ref[2 * g + e], NT_DIMS, preferred_element_type=F32)
            m = jnp.max(s, axis=-1, keepdims=True)
            p = jnp.exp(s - m)
            l = jnp.sum(p, axis=-1, keepdims=True)
            oe = jnp.dot(p.astype(BF16), vx_ref[2 * g + e], preferred_element_type=F32) / l
            o = oe if o is None else o + oe
        o_ref[:, pp * LANES:(pp + 1) * LANES] = o.astype(BF16)


def _attn(q, k, v, tq):
    b, t, _ = q.shape
    tk = k.shape[1]
    return pl.pallas_call(
        _attn_kernel,
        grid=(b, t // tq),
        in_specs=[pl.BlockSpec((None, tq, D_MODEL), lambda bb, i: (bb, i, 0)),
                  pl.BlockSpec((None, tk, KV_DIM), lambda bb, i: (bb, 0, 0)),
                  pl.BlockSpec((None, tk, KV_DIM), lambda bb, i: (bb, 0, 0))],
        out_specs=pl.BlockSpec((None, tq, D_MODEL), lambda bb, i: (bb, i, 0)),
        out_shape=jax.ShapeDtypeStruct((b, t, D_MODEL), BF16),
        scratch_shapes=[pltpu.VMEM((2 * N_KV_HEADS, tk, LANES), BF16),
                        pltpu.VMEM((2 * N_KV_HEADS, tk, LANES), BF16)],
        compiler_params=_cparams(("arbitrary", "arbitrary"), 48),
        name="attn",
    )(q, k, v)


def _merge_kernel(yf_ref, yb_ref, xs_ref, z_ref, o_ref, ga_ref, gb_ref, x_ref, mod_ref,
                  dsk_ref, snw_ref, wso_ref, wao_ref, wo_ref, x1_ref):
    y = yf_ref[...].astype(F32) + yb_ref[...].astype(F32) + dsk_ref[...] * xs_ref[...].astype(F32)
    y = y * _silu(z_ref[...].astype(F32))
    y = y * lax.rsqrt(jnp.mean(y * y, axis=-1, keepdims=True) + EPS) * snw_ref[...]
    ya = jnp.dot(y.astype(BF16), wso_ref[...], preferred_element_type=F32)
    yb2 = jnp.dot(o_ref[...], wao_ref[...], preferred_element_type=F32)
    merged = _sigmoid(ga_ref[...].astype(F32)) * ya + _sigmoid(gb_ref[...].astype(F32)) * yb2
    mix = jnp.dot(merged.astype(BF16), wo_ref[...], preferred_element_type=F32)
    gate1 = mod_ref[:, 2 * D_MODEL:3 * D_MODEL]
    x1_ref[...] = x_ref[...] + gate1 * mix


def _merge(y2, xbc, main, o, x, mod, dsk, snw, wso, wao, wo, tm):
    bm, t, d = x.shape
    tok = lambda w, cb: pl.BlockSpec((None, tm, w), lambda b, i: (b, i, cb))
    full = lambda a: pl.BlockSpec(a.shape, lambda b, i: (0,) * a.ndim)
    return pl.pallas_call(
        _merge_kernel,
        grid=(bm, t // tm),
        in_specs=[pl.BlockSpec((None, None, tm, SSD_D_INNER), lambda b, i: (0, b, i, 0)),
                  pl.BlockSpec((None, None, tm, SSD_D_INNER), lambda b, i: (1, b, i, 0)),
                  tok(SSD_D_INNER, 0), tok(SSD_D_INNER, COL_Z), tok(d, 0),
                  tok(d, COL_GA), tok(d, COL_GB), tok(d, 0),
                  pl.BlockSpec((None, 1, 6 * d), lambda b, i: (b, 0, 0)),
                  full(dsk), full(snw), full(wso), full(wao), full(wo)],
        out_specs=tok(d, 0),
        out_shape=jax.ShapeDtypeStruct((bm, t, d), F32),
        compiler_params=_cparams(("arbitrary", "arbitrary"), 48),
        name="merge",
    )(y2, y2, xbc, main, o, main, main, x, mod, dsk, snw, wso, wao, wo)


def _top_rows(s, k):
    rows = []
    for _ in range(k):
        m = jnp.max(s, axis=0, keepdims=True)
        rows.append(m)
        s = jnp.where(s == m, -jnp.inf, s)
    return rows


def _peer_front_kernel(x_ref, mod_ref, nw_ref, wpq_ref, keys_ref, h_ref, a1_ref, s1_ref, s2_ref, e2_ref, thr_ref):
    x = x_ref[...]
    r = lax.rsqrt(jnp.mean(x * x, axis=-1, keepdims=True) + EPS)
    shift = mod_ref[:, 3 * D_MODEL:4 * D_MODEL]
    scale = mod_ref[:, 4 * D_MODEL:5 * D_MODEL]
    hb = ((x * r * nw_ref[...]) * (1.0 + scale) + shift).astype(BF16)
    h_ref[...] = hb
    qp = jnp.dot(hb, wpq_ref[...], preferred_element_type=F32).astype(BF16)
    for hd in range(PEER_HEADS):
        s1 = lax.dot_general(keys_ref[2 * hd], qp[:, (2 * hd) * LANES:(2 * hd + 1) * LANES],
                             NT_DIMS, preferred_element_type=F32)
        s2 = lax.dot_general(keys_ref[2 * hd + 1], qp[:, (2 * hd + 1) * LANES:(2 * hd + 2) * LANES],
                             NT_DIMS, preferred_element_type=F32)
        ta = _top_rows(s1, PEER_TOPK)
        tb = jnp.concatenate(_top_rows(s2, PEER_TOPK), axis=0)
        cand = jnp.concatenate([ta[p] + tb for p in range(PEER_TOPK)], axis=0)
        best = _top_rows(cand, PEER_TOPK)
        zsum = jnp.ones_like(best[0])
        for v in best[1:]:
            zsum = zsum + jnp.exp(v - best[0])
        thr_ref[hd:hd + 1, :] = best[-1]
        a1_ref[hd] = jnp.exp(s1 - ta[0])
        s1_ref[hd] = s1
        s2_ref[hd] = s2
        e2_ref[hd] = jnp.exp(s2 - tb[0:1]) / zsum


def _peer_front(x1, mod, nw, wpq, keys, tm):
    bm, t, d = x1.shape
    sel = lambda: pl.BlockSpec((None, PEER_HEADS, N_KEYS, tm), lambda b, i: (b, 0, 0, i))
    sel_shape = jax.ShapeDtypeStruct((bm, PEER_HEADS, N_KEYS, t), F32)
    return pl.pallas_call(
        _peer_front_kernel,
        grid=(bm, t // tm),
        in_specs=[pl.BlockSpec((None, tm, d), lambda b, i: (b, i, 0)),
                  pl.BlockSpec((None, 1, 6 * d), lambda b, i: (b, 0, 0)),
                  pl.BlockSpec((1, d), lambda b, i: (0, 0)),
                  pl.BlockSpec(wpq.shape, lambda b, i: (0, 0)),
                  pl.BlockSpec(keys.shape, lambda b, i: (0, 0, 0))],
        out_specs=[pl.BlockSpec((None, tm, d), lambda b, i: (b, i, 0)), sel(), sel(), sel(), sel(),
                   pl.BlockSpec((None, PEER_HEADS, tm), lambda b, i: (b, 0, i))],
        out_shape=[jax.ShapeDtypeStruct((bm, t, d), BF16), sel_shape, sel_shape, sel_shape, sel_shape,
                   jax.ShapeDtypeStruct((bm, PEER_HEADS, t), F32)],
        compiler_params=_cparams(("arbitrary", "arbitrary"), 48),
        name="peer_front",
    )(x1, mod, nw, wpq, keys)


PEER_EB = 256


def _gelu_tanh(x):
    return 0.5 * x * (1.0 + jnp.tanh(0.7978845608028654 * (x + 0.044715 * (x * x * x))))


def _peer_dense_kernel(h_ref, a1_ref, s1_ref, s2_ref, e2_ref, thr_ref, u_ref, vt_ref, x1_ref, mod_ref, fw_ref,
                       y_ref, acc_ref, *, ne):
    e = pl.program_id(2)

    @pl.when(e == 0)
    def _():
        acc_ref[...] = jnp.zeros_like(acc_ref)

    h = h_ref[...]
    parts = []
    for sub in range(PEER_EB // N_KEYS):
        i = e * (PEER_EB // N_KEYS) + sub
        act = lax.dot_general(u_ref[sub * N_KEYS:(sub + 1) * N_KEYS, :], h, NT_DIMS,
                              preferred_element_type=F32)
        w = None
        for hd in range(PEER_HEADS):
            a1 = a1_ref[hd, pl.ds(i, 1), :]
            s1 = s1_ref[hd, pl.ds(i, 1), :]
            wh = a1 * jnp.where(s1 + s2_ref[hd] >= thr_ref[hd:hd + 1, :], e2_ref[hd], 0.0)
            w = wh if w is None else w + wh
        parts.append((w * _gelu_tanh(act)).astype(BF16))
    gt = jnp.concatenate(parts, axis=0)
    acc_ref[...] += jnp.dot(vt_ref[...], gt, preferred_element_type=F32)

    @pl.when(e == ne - 1)
    def _():
        gate2 = mod_ref[:, 5 * D_MODEL:6 * D_MODEL]
        x2 = x1_ref[...] + gate2 * acc_ref[...].T
        y_ref[...] = x2 * lax.rsqrt(jnp.mean(x2 * x2, axis=-1, keepdims=True) + EPS) * fw_ref[...]


def _peer_dense(h2, a1, s1, s2, e2, thr, u, vt, x1, mod, fw, tt):
    bm, t, d = x1.shape
    ne = u.shape[0] // PEER_EB
    sel = lambda: pl.BlockSpec((None, PEER_HEADS, N_KEYS, tt), lambda b, i, e: (b, 0, 0, i))
    return pl.pallas_call(
        functools.partial(_peer_dense_kernel, ne=ne),
        grid=(bm, t // tt, ne),
        in_specs=[pl.BlockSpec((None, tt, d), lambda b, i, e: (b, i, 0)),
                  sel(), sel(), sel(), sel(),
                  pl.BlockSpec((None, PEER_HEADS, tt), lambda b, i, e: (b, 0, i)),
                  pl.BlockSpec((PEER_EB, d), lambda b, i, e: (e, 0)),
                  pl.BlockSpec((d, PEER_EB), lambda b, i, e: (0, e)),
                  pl.BlockSpec((None, tt, d), lambda b, i, e: (b, i, 0)),
                  pl.BlockSpec((None, 1, 6 * d), lambda b, i, e: (b, 0, 0)),
                  pl.BlockSpec((1, d), lambda b, i, e: (0, 0))],
        out_specs=pl.BlockSpec((None, tt, d), lambda b, i, e: (b, i, 0)),
        out_shape=jax.ShapeDtypeStruct((bm, t, d), F32),
        scratch_shapes=[pltpu.VMEM((d, tt), F32)],
        compiler_params=_cparams(("arbitrary", "arbitrary", "arbitrary"), 56),
        name="peer_dense",
    )(h2, a1, s1, s2, e2, thr, u, vt, x1, mod, fw)


def _rope_tables(t):
    pos_t = np.arange(t)
    j = np.arange(HEAD_DIM)
    blk = j // (HEAD_DIM // 2)
    jj = j % (HEAD_DIM // 2)
    half = HEAD_DIM // 4
    inv = ROPE_THETA ** (-(jj % half).astype(np.float32) / half)
    pos = jnp.where(blk[None, :] == 0, (pos_t // GRID_W)[:, None], (pos_t % GRID_W)[:, None]).astype(F32)
    ang = pos * jnp.asarray(inv, F32)[None, :]
    cos, sin = jnp.cos(ang), jnp.sin(ang)
    first = jnp.asarray(jj < half)[None, :]
    sin_a = jnp.where(first, -sin, 0.0)
    sin_b = jnp.where(first, 0.0, sin)
    rep = lambda a: jnp.concatenate([a, a], axis=1)
    return rep(cos), rep(sin_a), rep(sin_b)


def _block_ones(n):
    idx = np.arange(n) // HEAD_DIM
    return jnp.asarray(idx[:, None] == idx[None, :], BF16)


def _group(x, mod, init, ctx_kv, seq_shape, wts, tables):
    bm, tmod, d = x.shape
    b, t = seq_shape
    seq = lambda a: a.reshape((b, t) + a.shape[2:])
    tok = lambda a: a.reshape((bm, tmod) + a.shape[2:])

    main, small = _inproj(x, mod, wts["norm1_w"], wts["w_cat"], 512)
    xbc = _conv(seq(main), wts["conv_w"], wts["conv_b"], 256)
    y2, fin = _ssd(xbc, seq(small), wts["ssd_prm"], init)
    qn, kn, vn = _prep(seq(main), seq(small), wts["qnw_t"], wts["knw_t"], wts["bq"], wts["bk"], tables, 256)
    if ctx_kv is None:
        keys, vals = kn, vn
    else:
        keys = jnp.concatenate([kn, ctx_kv[0]], axis=1)
        vals = jnp.concatenate([vn, ctx_kv[1]], axis=1)
    o = _attn(qn, keys, vals, 256)
    y2m = y2.reshape((2, bm, tmod, SSD_D_INNER))
    x1 = _merge(y2m, tok(xbc), main, tok(o), x, mod, wts["dsk"], wts["ssd_norm_w"],
                wts["w_ssd_out"], wts["w_attn_out"], wts["w_o"], 256)
    h2, a1, s1, s2, e2, thr = _peer_front(x1, mod, wts["norm2_w"], wts["w_pq"], wts["peer_keys"], 256)
    y = _peer_dense(h2, a1, s1, s2, e2, thr, wts["peer_u"], wts["peer_vt"], x1, mod, wts["final_norm_w"], 512)
    return y, fin, kn, vn


def kernel(x_prompt, x_sample, cache_k, cache_v, state_ssm_fwd, state_ssm_bwd, c, c_ctx, w_ada, b_ada, norm1_w, norm2_w, w_in, conv_w, conv_b, a_log_f, a_log_b, dt_bias_f, dt_bias_b, d_skip, ssd_norm_w, w_ssd_out, q_norm_w, k_norm_w, w_attn_out, w_o, w_pq, peer_keys, peer_u, peer_v, final_norm_w):
    assert w_in.shape[0] == 1, "single-layer problem"
    bp, tp, d = x_prompt.shape
    bs, ts, _ = x_sample.shape

    wi = w_in[0]
    offs = np.cumsum([0, SSD_D_INNER, XBC_DIM, SSD_HEADS, SSD_HEADS, D_MODEL, KV_DIM, KV_DIM, D_MODEL, D_MODEL])
    part = lambda n: wi[:, offs[n]:offs[n + 1]]
    zpad = lambda n: jnp.zeros((d, n), F32)
    w_cat = jnp.concatenate(
        [part(0), part(1), part(4), part(7), part(8),
         part(5), part(6),
         part(2), zpad(LANES - SSD_HEADS), part(3), zpad(LANES - SSD_HEADS),
         zpad(SMALL_W - 2 * KV_DIM - 2 * LANES)], axis=1).astype(BF16)
    row = lambda a: a.reshape(1, -1).astype(F32)
    padl = lambda a: jnp.pad(a.astype(F32), (0, LANES - a.shape[0]))
    prm = jnp.zeros((2, 8, LANES), F32)
    prm = prm.at[0, 0].set(padl(a_log_f[0])).at[0, 1].set(padl(dt_bias_f[0]))
    prm = prm.at[1, 0].set(padl(a_log_b[0])).at[1, 1].set(padl(dt_bias_b[0]))
    wts = {
        "norm1_w": row(norm1_w[0]), "norm2_w": row(norm2_w[0]), "w_cat": w_cat,
        "conv_w": jnp.pad(conv_w[0], ((0, 8 - CONV_W), (0, 0))), "conv_b": row(conv_b[0]),
        "ssd_prm": prm,
        "dsk": row(jnp.repeat(d_skip[0], SSD_HEADDIM)), "ssd_norm_w": row(ssd_norm_w[0]),
        "w_ssd_out": w_ssd_out[0].astype(BF16), "w_attn_out": w_attn_out[0].astype(BF16),
        "w_o": w_o[0].astype(BF16),
        "qnw_t": row(jnp.tile(q_norm_w[0], N_HEADS)), "knw_t": row(jnp.tile(k_norm_w[0], N_KV_HEADS)),
        "bq": _block_ones(D_MODEL), "bk": _block_ones(KV_DIM),
        "w_pq": w_pq[0].astype(BF16),
        "peer_keys": peer_keys[0].reshape(PEER_HEADS * 2, N_KEYS, PEER_DQ // 2).astype(BF16),
        "peer_u": peer_u[0].astype(BF16), "peer_vt": peer_v[0].T.astype(BF16),
        "final_norm_w": row(final_norm_w),
    }

    cc = jnp.concatenate([c_ctx[None, :], c, jnp.zeros((8 - 1 - bs, d), F32)], axis=0)
    mod = _ada(cc, w_ada[0], row(b_ada[0]))
    mod_p = mod[0:1].reshape(1, 1, 6 * d)
    mod_s = mod[1:1 + bs].reshape(bs, 1, 6 * d)

    yp, fin_p, kn_p, vn_p = _group(x_prompt.reshape(1, bp * tp, d), mod_p, None, None, (bp, tp), wts, None)

    init = jnp.stack([state_ssm_fwd[:, 0], state_ssm_bwd[:, 0]], axis=0).reshape(2, bs, SSD_D_INNER, D_STATE)
    ctx_kv = (cache_k[:, 0].reshape(bs, -1, KV_DIM), cache_v[:, 0].reshape(bs, -1, KV_DIM))
    ys, _, _, _ = _group(x_sample, mod_s, init, ctx_kv, (bs, ts), wts, _rope_tables(ts))

    y_prompt = yp.reshape(bp, tp, d)
    new_cache_k = kn_p.reshape(bp, 1, tp, N_KV_HEADS, HEAD_DIM)
    new_cache_v = vn_p.reshape(bp, 1, tp, N_KV_HEADS, HEAD_DIM)
    new_state_fwd = fin_p[0].reshape(bp, 1, SSD_HEADS, SSD_HEADDIM, D_STATE)
    new_state_bwd = fin_p[1].reshape(bp, 1, SSD_HEADS, SSD_HEADDIM, D_STATE)
    return (y_prompt, ys, new_cache_k, new_cache_v, new_state_fwd, new_state_bwd)
```

```python
import functools

import jax
import jax.numpy as jnp
import numpy as np
from jax import lax
from jax.experimental import pallas as pl
from jax.experimental.pallas import tpu as pltpu

F32 = jnp.float32
BF16 = jnp.bfloat16

D_MODEL = 1024
GRID_W = 64
SSD_D_INNER = 2048
SSD_HEADDIM = 64
SSD_HEADS = 32
SSD_GROUPS = 8
D_STATE = 128
CONV_W = 5
CHUNK = 128
XBC_DIM = 4096
N_HEADS = 16
N_KV_HEADS = 4
HEAD_DIM = 64
KV_DIM = N_KV_HEADS * HEAD_DIM
ROPE_THETA = 10000.0
N_KEYS = 128
PEER_HEADS = 8
PEER_TOPK = 16
PEER_DQ = 256
EPS = 1e-6

LANES = 128
MIB = 1024 * 1024

IN_TN = 1024
COL_Z = 0
COL_XBC = 2
COL_Q = 6
COL_GA = 7
COL_GB = 8
N_MAIN = 9
MAIN_W = N_MAIN * IN_TN
SMALL_W = IN_TN

NT_DIMS = (((1,), (1,)), ((), ()))


def _cparams(sem, vmem_mib):
    return pltpu.CompilerParams(dimension_semantics=sem, vmem_limit_bytes=vmem_mib * MIB)


def _sigmoid(x):
    return 1.0 / (1.0 + jnp.exp(-x))


def _silu(x):
    return x * _sigmoid(x)


def _softplus(x):
    return jnp.maximum(x, 0.0) + jnp.log(1.0 + jnp.exp(-jnp.abs(x)))


def _ada_kernel(c_ref, w_ref, b_ref, o_ref):
    sc = _silu(c_ref[...])
    o_ref[...] = jnp.dot(sc.astype(BF16), w_ref[...].astype(BF16),
                         preferred_element_type=F32) + b_ref[...]


def _ada(cc, w, b):
    rows, d = cc.shape
    n = w.shape[1]
    tn = 1536
    return pl.pallas_call(
        _ada_kernel,
        grid=(n // tn,),
        in_specs=[pl.BlockSpec((rows, d), lambda j: (0, 0)),
                  pl.BlockSpec((d, tn), lambda j: (0, j)),
                  pl.BlockSpec((1, tn), lambda j: (0, j))],
        out_specs=pl.BlockSpec((rows, tn), lambda j: (0, j)),
        out_shape=jax.ShapeDtypeStruct((rows, n), F32),
        compiler_params=_cparams(("arbitrary",), 32),
        name="ada",
    )(cc, w, b)


def _inproj_kernel(x_ref, mod_ref, nw_ref, w_ref, main_ref, small_ref, h_ref):
    j = pl.program_id(2)

    @pl.when(j == 0)
    def _():
        x = x_ref[...]
        r = lax.rsqrt(jnp.mean(x * x, axis=-1, keepdims=True) + EPS)
        shift = mod_ref[:, 0:D_MODEL]
        scale = mod_ref[:, D_MODEL:2 * D_MODEL]
        h_ref[...] = ((x * r * nw_ref[...]) * (1.0 + scale) + shift).astype(BF16)

    acc = jnp.dot(h_ref[...], w_ref[...], preferred_element_type=F32)

    @pl.when(j < N_MAIN)
    def _():
        main_ref[...] = acc.astype(BF16)

    @pl.when(j == N_MAIN)
    def _():
        small_ref[...] = acc


def _inproj(x, mod, nw, w_cat, tm):
    bm, t, d = x.shape
    return pl.pallas_call(
        _inproj_kernel,
        grid=(bm, t // tm, N_MAIN + 1),
        in_specs=[pl.BlockSpec((None, tm, d), lambda b, i, j: (b, i, 0)),
                  pl.BlockSpec((None, 1, 6 * d), lambda b, i, j: (b, 0, 0)),
                  pl.BlockSpec((1, d), lambda b, i, j: (0, 0)),
                  pl.BlockSpec((d, IN_TN), lambda b, i, j: (0, j))],
        out_specs=[pl.BlockSpec((None, tm, IN_TN), lambda b, i, j: (b, i, jnp.minimum(j, N_MAIN - 1))),
                   pl.BlockSpec((None, tm, SMALL_W), lambda b, i, j: (b, i, 0))],
        out_shape=[jax.ShapeDtypeStruct((bm, t, MAIN_W), BF16),
                   jax.ShapeDtypeStruct((bm, t, SMALL_W), F32)],
        scratch_shapes=[pltpu.VMEM((tm, d), BF16)],
        compiler_params=_cparams(("arbitrary", "arbitrary", "arbitrary"), 40),
        name="inproj",
    )(x, mod, nw, w_cat)


CONV_HALO = 16


def _conv_kernel(cur_ref, prev_ref, next_ref, w_ref, b_ref, o_ref, *, nt, tt):
    i = pl.program_id(1)
    cur = cur_ref[...].astype(F32)
    pv = prev_ref[...].astype(F32)[8:16]
    nx = next_ref[...].astype(F32)[0:8]
    pv = jnp.where(i > 0, pv, 0.0)
    nx = jnp.where(i < nt - 1, nx, 0.0)
    ext = jnp.concatenate([pv, cur, nx], axis=0)
    acc = jnp.broadcast_to(b_ref[...], cur.shape)
    for k in range(CONV_W):
        off = 8 + k - CONV_W // 2
        acc = acc + w_ref[k:k + 1, :] * ext[off:off + tt]
    o_ref[...] = _silu(acc).astype(BF16)


def _conv(main, conv_w, conv_b, tt):
    b, t, _ = main.shape
    tc = IN_TN
    nt = t // tt
    hb = tt // CONV_HALO
    last = t // CONV_HALO - 1
    return pl.pallas_call(
        functools.partial(_conv_kernel, nt=nt, tt=tt),
        grid=(b, nt, XBC_DIM // tc),
        in_specs=[pl.BlockSpec((None, tt, tc), lambda bb, i, j: (bb, i, COL_XBC + j)),
                  pl.BlockSpec((None, CONV_HALO, tc),
                               lambda bb, i, j: (bb, jnp.maximum(i * hb - 1, 0), COL_XBC + j)),
                  pl.BlockSpec((None, CONV_HALO, tc),
                               lambda bb, i, j: (bb, jnp.minimum((i + 1) * hb, last), COL_XBC + j)),
                  pl.BlockSpec((8, tc), lambda bb, i, j: (0, j)),
                  pl.BlockSpec((1, tc), lambda bb, i, j: (0, j))],
        out_specs=pl.BlockSpec((None, tt, tc), lambda bb, i, j: (bb, i, j)),
        out_shape=jax.ShapeDtypeStruct((b, t, XBC_DIM), BF16),
        compiler_params=_cparams(("arbitrary", "arbitrary", "arbitrary"), 32),
        name="conv",
    )(main, main, main, conv_w, conv_b)


N_PAIRS = SSD_HEADS // 2


def _ssd_kernel(*refs, nc, has_init):
    if has_init:
        xs_ref, b_ref, c_ref, dt_ref, prm_ref, init_ref, y_ref, fin_ref, st_ref = refs
    else:
        xs_ref, b_ref, c_ref, dt_ref, prm_ref, y_ref, fin_ref, st_ref = refs
    d = pl.program_id(0)
    c = pl.program_id(2)

    @pl.when(c == 0)
    def _():
        for pp in range(N_PAIRS):
            if has_init:
                st_ref[pp] = init_ref[pp * LANES:(pp + 1) * LANES, :].T
            else:
                st_ref[pp] = jnp.zeros((D_STATE, LANES), F32)

    row = lax.broadcasted_iota(jnp.int32, (CHUNK, CHUNK), 0)
    col = lax.broadcasted_iota(jnp.int32, (CHUNK, CHUNK), 1)
    tri = (row - col) * (1 - 2 * d) >= 0
    lo = col < SSD_HEADDIM

    a_log = prm_ref[0:1, :]
    bias = prm_ref[1:2, :]
    dt = _softplus(dt_ref[...] + bias)
    a = dt * (-jnp.exp(a_log))
    cs = jnp.dot(tri.astype(F32), a, precision=lax.Precision.HIGHEST,
                 preferred_element_type=F32)
    tot = jnp.sum(a, axis=0, keepdims=True)
    w_t = (jnp.exp(tot - cs) * dt).T
    r_t = (cs - jnp.log(dt)).T
    dec = jnp.exp(tot)

    for g in range(SSD_GROUPS):
        bg = b_ref[:, g * D_STATE:(g + 1) * D_STATE]
        cg = c_ref[:, g * D_STATE:(g + 1) * D_STATE]
        sc = lax.dot_general(cg, bg, NT_DIMS, preferred_element_type=F32)
        bg_t = bg.astype(F32).T
        cgf = cg.astype(F32)
        for q in range(2):
            pp = g * 2 + q
            xp = xs_ref[:, pp * LANES:(pp + 1) * LANES]
            st_old = st_ref[pp]
            rhs = jnp.concatenate([xp, st_old.astype(BF16)], axis=0)
            ys, us = [], []
            for e in range(2):
                h = pp * 2 + e
                csb = jnp.broadcast_to(cs[:, h:h + 1], (CHUNK, CHUNK))
                m = sc * jnp.exp(jnp.where(tri, csb - r_t[h:h + 1, :], -jnp.inf))
                cse = cgf * jnp.exp(csb)
                lhs = jnp.concatenate([m.astype(BF16), cse.astype(BF16)], axis=1)
                ys.append(jnp.dot(lhs, rhs, preferred_element_type=F32))
                bw = (bg_t * w_t[h:h + 1, :]).astype(BF16)
                us.append(jnp.dot(bw, xp, preferred_element_type=F32))
            y_ref[:, pp * LANES:(pp + 1) * LANES] = jnp.where(lo, ys[0], ys[1]).astype(BF16)
            decrow = jnp.where(lo[0:1, :], dec[:, 2 * pp:2 * pp + 1], dec[:, 2 * pp + 1:2 * pp + 2])
            st_ref[pp] = st_old * decrow + jnp.where(lo, us[0], us[1])

    @pl.when(c == nc - 1)
    def _():
        for pp in range(N_PAIRS):
            fin_ref[pp * LANES:(pp + 1) * LANES, :] = st_ref[pp].T


def _ssd(xbc, small, prm, init):
    b, t, _ = xbc.shape
    nc = t // CHUNK
    has_init = init is not None

    def ce(d, c):
        return c + d * (nc - 1 - 2 * c)

    in_specs = [pl.BlockSpec((None, CHUNK, SSD_D_INNER), lambda d, bb, c: (bb, ce(d, c), 0)),
                pl.BlockSpec((None, CHUNK, SSD_GROUPS * D_STATE), lambda d, bb, c: (bb, ce(d, c), 2)),
                pl.BlockSpec((None, CHUNK, SSD_GROUPS * D_STATE), lambda d, bb, c: (bb, ce(d, c), 3)),
                pl.BlockSpec((None, CHUNK, LANES), lambda d, bb, c: (bb, ce(d, c), 4 + d)),
                pl.BlockSpec((None, 8, LANES), lambda d, bb, c: (d, 0, 0))]
    args = [xbc, xbc, xbc, small, prm]
    if has_init:
        in_specs.append(pl.BlockSpec((None, None, SSD_D_INNER, D_STATE), lambda d, bb, c: (d, bb, 0, 0)))
        args.append(init)
    return pl.pallas_call(
        functools.partial(_ssd_kernel, nc=nc, has_init=has_init),
        grid=(2, b, nc),
        in_specs=in_specs,
        out_specs=[pl.BlockSpec((None, None, CHUNK, SSD_D_INNER), lambda d, bb, c: (d, bb, ce(d, c), 0)),
                   pl.BlockSpec((None, None, SSD_D_INNER, D_STATE), lambda d, bb, c: (d, bb, 0, 0))],
        out_shape=[jax.ShapeDtypeStruct((2, b, t, SSD_D_INNER), BF16),
                   jax.ShapeDtypeStruct((2, b, SSD_D_INNER, D_STATE), F32)],
        scratch_shapes=[pltpu.VMEM((N_PAIRS, D_STATE, LANES), F32)],
        compiler_params=_cparams(("arbitrary", "arbitrary", "arbitrary"), 40),
        name="ssd",
    )(*args)


def _rope(x, cos, sin_a, sin_b, reps):
    n = x.shape[1]
    half = HEAD_DIM // 4
    cosf = jnp.concatenate([cos] * reps, axis=1)
    saf = jnp.concatenate([sin_a] * reps, axis=1)
    sbf = jnp.concatenate([sin_b] * reps, axis=1)
    return x * cosf + pltpu.roll(x, n - half, 1) * saf + pltpu.roll(x, half, 1) * sbf


def _prep_kernel(*refs, rope):
    if rope:
        q_ref, kv_ref, qnw_ref, knw_ref, bq_ref, bk_ref, cos_ref, sa_ref, sb_ref, qo_ref, ko_ref, vo_ref = refs
    else:
        q_ref, kv_ref, qnw_ref, knw_ref, bq_ref, bk_ref, qo_ref, ko_ref, vo_ref = refs
    q = q_ref[...].astype(F32)
    k = kv_ref[:, 0:KV_DIM]
    vo_ref[...] = kv_ref[:, KV_DIM:2 * KV_DIM]
    qss = jnp.dot((q * q).astype(BF16), bq_ref[...], preferred_element_type=F32)
    kss = jnp.dot((k * k).astype(BF16), bk_ref[...], preferred_element_type=F32)
    qn = q * lax.rsqrt(qss * (1.0 / HEAD_DIM) + EPS) * qnw_ref[...]
    kn = k * lax.rsqrt(kss * (1.0 / HEAD_DIM) + EPS) * knw_ref[...]
    if rope:
        cos, sa, sb = cos_ref[...], sa_ref[...], sb_ref[...]
        qn = _rope(qn, cos, sa, sb, D_MODEL // LANES)
        kn = _rope(kn, cos, sa, sb, KV_DIM // LANES)
    qo_ref[...] = (qn * (HEAD_DIM ** -0.5)).astype(BF16)
    ko_ref[...] = kn


def _prep(main, small, qnw_t, knw_t, bq, bk, tables, tt):
    b, t, _ = main.shape
    rope = tables is not None
    in_specs = [pl.BlockSpec((None, tt, D_MODEL), lambda bb, i: (bb, i, COL_Q)),
                pl.BlockSpec((None, tt, 2 * KV_DIM), lambda bb, i: (bb, i, 0)),
                pl.BlockSpec((1, D_MODEL), lambda bb, i: (0, 0)),
                pl.BlockSpec((1, KV_DIM), lambda bb, i: (0, 0)),
                pl.BlockSpec((D_MODEL, D_MODEL), lambda bb, i: (0, 0)),
                pl.BlockSpec((KV_DIM, KV_DIM), lambda bb, i: (0, 0))]
    args = [main, small, qnw_t, knw_t, bq, bk]
    if rope:
        in_specs += [pl.BlockSpec((tt, LANES), lambda bb, i: (i, 0))] * 3
        args += list(tables)
    return pl.pallas_call(
        functools.partial(_prep_kernel, rope=rope),
        grid=(b, t // tt),
        in_specs=in_specs,
        out_specs=[pl.BlockSpec((None, tt, D_MODEL), lambda bb, i: (bb, i, 0)),
                   pl.BlockSpec((None, tt, KV_DIM), lambda bb, i: (bb, i, 0)),
                   pl.BlockSpec((None, tt, KV_DIM), lambda bb, i: (bb, i, 0))],
        out_shape=[jax.ShapeDtypeStruct((b, t, D_MODEL), BF16),
                   jax.ShapeDtypeStruct((b, t, KV_DIM), F32),
                   jax.ShapeDtypeStruct((b, t, KV_DIM), F32)],
        compiler_params=_cparams(("arbitrary", "arbitrary"), 32),
        name="qk_prep",
    )(*args)


def _attn_kernel(q_ref, k_ref, v_ref, o_ref, kx_ref, vx_ref):
    qi = pl.program_id(1)
    tk = k_ref.shape[0]

    @pl.when(qi == 0)
    def _():
        lo = lax.broadcasted_iota(jnp.int32, (tk, LANES), 1) < HEAD_DIM
        for pr in range(KV_DIM // LANES):
            for src, dst in ((k_ref, kx_ref), (v_ref, vx_ref)):
                a = src[:, pr * LANES:(pr + 1) * LANES].astype(F32)
                ar = pltpu.roll(a, HEAD_DIM, 1)
                dst[4 * pr + 0] = jnp.where(lo, a, 0.0).astype(BF16)
                dst[4 * pr + 1] = jnp.where(lo, 0.0, ar).astype(BF16)
                dst[4 * pr + 2] = jnp.where(lo, ar, 0.0).astype(BF16)
                dst[4 * pr + 3] = jnp.where(lo, 0.0, a).astype(BF16)

    for pp in range(N_HEADS // 2):
        g = pp // 2
        qp = q_ref[:, pp * LANES:(pp + 1) * LANES]
        o = None
        for e in range(2):
            s = lax.dot_general(qp, kx_ref[2 * g + e], NT_DIMS, preferred_element_type=F32)
            m = jnp.max(s, axis=-1, keepdims=True)
            p = jnp.exp(s - m)
            l = jnp.sum(p, axis=-1, keepdims=True)
            oe = jnp.dot(p.astype(BF16), vx_ref[2 * g + e], preferred_element_type=F32) / l
            o = oe if o is None else o + oe
        o_ref[:, pp * LANES:(pp + 1) * LANES] = o.astype(BF16)


def _attn(q, k, v, tq):
    b, t, _ = q.shape
    tk = k.shape[1]
    return pl.pallas_call(
        _attn_kernel,
        grid=(b, t // tq),
        in_specs=[pl.BlockSpec((None, tq, D_MODEL), lambda bb, i: (bb, i, 0)),
                  pl.BlockSpec((None, tk, KV_DIM), lambda bb, i: (bb, 0, 0)),
                  pl.BlockSpec((None, tk, KV_DIM), lambda bb, i: (bb, 0, 0))],
        out_specs=pl.BlockSpec((None, tq, D_MODEL), lambda bb, i: (bb, i, 0)),
        out_shape=jax.ShapeDtypeStruct((b, t, D_MODEL), BF16),
        scratch_shapes=[pltpu.VMEM((2 * N_KV_HEADS, tk, LANES), BF16),
                        pltpu.VMEM((2 * N_KV_HEADS, tk, LANES), BF16)],
        compiler_params=_cparams(("arbitrary", "arbitrary"), 48),
        name="attn",
    )(q, k, v)


def _merge_kernel(yf_ref, yb_ref, xs_ref, z_ref, o_ref, ga_ref, gb_ref, x_ref, mod_ref,
                  dsk_ref, snw_ref, wso_ref, wao_ref, wo_ref, x1_ref):
    y = yf_ref[...].astype(F32) + yb_ref[...].astype(F32) + dsk_ref[...] * xs_ref[...].astype(F32)
    y = y * _silu(z_ref[...].astype(F32))
    y = y * lax.rsqrt(jnp.mean(y * y, axis=-1, keepdims=True) + EPS) * snw_ref[...]
    ya = jnp.dot(y.astype(BF16), wso_ref[...], preferred_element_type=F32)
    yb2 = jnp.dot(o_ref[...], wao_ref[...], preferred_element_type=F32)
    merged = _sigmoid(ga_ref[...].astype(F32)) * ya + _sigmoid(gb_ref[...].astype(F32)) * yb2
    mix = jnp.dot(merged.astype(BF16), wo_ref[...], preferred_element_type=F32)
    gate1 = mod_ref[:, 2 * D_MODEL:3 * D_MODEL]
    x1_ref[...] = x_ref[...] + gate1 * mix


def _merge(y2, xbc, main, o, x, mod, dsk, snw, wso, wao, wo, tm):
    bm, t, d = x.shape
    tok = lambda w, cb: pl.BlockSpec((None, tm, w), lambda b, i: (b, i, cb))
    full = lambda a: pl.BlockSpec(a.shape, lambda b, i: (0,) * a.ndim)
    return pl.pallas_call(
        _merge_kernel,
        grid=(bm, t // tm),
        in_specs=[pl.BlockSpec((None, None, tm, SSD_D_INNER), lambda b, i: (0, b, i, 0)),
                  pl.BlockSpec((None, None, tm, SSD_D_INNER), lambda b, i: (1, b, i, 0)),
                  tok(SSD_D_INNER, 0), tok(SSD_D_INNER, COL_Z), tok(d, 0),
                  tok(d, COL_GA), tok(d, COL_GB), tok(d, 0),
                  pl.BlockSpec((None, 1, 6 * d), lambda b, i: (b, 0, 0)),
                  full(dsk), full(snw), full(wso), full(wao), full(wo)],
        out_specs=tok(d, 0),
        out_shape=jax.ShapeDtypeStruct((bm, t, d), F32),
        compiler_params=_cparams(("arbitrary", "arbitrary"), 48),
        name="merge",
    )(y2, y2, xbc, main, o, main, main, x, mod, dsk, snw, wso, wao, wo)


def _top_rows(s, k, ranks=False):
    rows = []
    rank = jnp.full(s.shape, float(k), F32) if ranks else None
    for it in range(k):
        m = jnp.max(s, axis=0, keepdims=True)
        rows.append(m)
        hit = s == m
        s = jnp.where(hit, -jnp.inf, s)
        if ranks:
            rank = jnp.where(hit, float(it), rank)
    return (rows, rank) if ranks else rows


def _peer_front_kernel(x_ref, mod_ref, nw_ref, wpq_ref, keys_ref, ht_ref, a1_ref, n1_ref, r2_ref, e2_ref):
    x = x_ref[...]
    r = lax.rsqrt(jnp.mean(x * x, axis=-1, keepdims=True) + EPS)
    shift = mod_ref[:, 3 * D_MODEL:4 * D_MODEL]
    scale = mod_ref[:, 4 * D_MODEL:5 * D_MODEL]
    h = (x * r * nw_ref[...]) * (1.0 + scale) + shift
    hb = h.astype(BF16)
    ht_ref[...] = h.T.astype(BF16)
    qp = jnp.dot(hb, wpq_ref[...], preferred_element_type=F32).astype(BF16)
    k = PEER_TOPK
    for hd in range(PEER_HEADS):
        s1 = lax.dot_general(keys_ref[2 * hd], qp[:, (2 * hd) * LANES:(2 * hd + 1) * LANES],
                             NT_DIMS, preferred_element_type=F32)
        s2 = lax.dot_general(keys_ref[2 * hd + 1], qp[:, (2 * hd + 1) * LANES:(2 * hd + 2) * LANES],
                             NT_DIMS, preferred_element_type=F32)
        ta = _top_rows(s1, k)
        tb_rows, rank2 = _top_rows(s2, k, ranks=True)
        tb = jnp.concatenate(tb_rows, axis=0)
        pieces = [ta[0] + tb] + [ta[p] + tb[0:8] for p in range(1, 8)]
        pieces.append(jnp.concatenate(ta[8:], axis=0) + tb[0:1])
        best = _top_rows(jnp.concatenate(pieces, axis=0), k)
        zsum = jnp.ones_like(best[0])
        for v in best[1:]:
            zsum = zsum + jnp.exp(v - best[0])
        thr = best[-1]
        n1 = jnp.zeros_like(s1)
        for q in range(k):
            n1 = n1 + jnp.where(s1 + tb_rows[q] >= thr, 1.0, 0.0)
        a1_ref[hd] = jnp.exp(s1 - ta[0])
        n1_ref[hd] = n1
        r2_ref[hd] = rank2.astype(BF16)
        e2_ref[hd] = (jnp.exp(s2 - tb_rows[0]) / zsum).astype(BF16)


def _peer_front(x1, mod, nw, wpq, keys, tm):
    bm, t, d = x1.shape
    sel = lambda: pl.BlockSpec((None, PEER_HEADS, N_KEYS, tm), lambda b, i: (b, 0, 0, i))
    sel_shape = lambda dt: jax.ShapeDtypeStruct((bm, PEER_HEADS, N_KEYS, t), dt)
    return pl.pallas_call(
        _peer_front_kernel,
        grid=(bm, t // tm),
        in_specs=[pl.BlockSpec((None, tm, d), lambda b, i: (b, i, 0)),
                  pl.BlockSpec((None, 1, 6 * d), lambda b, i: (b, 0, 0)),
                  pl.BlockSpec((1, d), lambda b, i: (0, 0)),
                  pl.BlockSpec(wpq.shape, lambda b, i: (0, 0)),
                  pl.BlockSpec(keys.shape, lambda b, i: (0, 0, 0))],
        out_specs=[pl.BlockSpec((None, d, tm), lambda b, i: (b, 0, i)), sel(), sel(), sel(), sel()],
        out_shape=[jax.ShapeDtypeStruct((bm, d, t), BF16), sel_shape(F32), sel_shape(F32),
                   sel_shape(BF16), sel_shape(BF16)],
        compiler_params=_cparams(("arbitrary", "arbitrary"), 48),
        name="peer_front",
    )(x1, mod, nw, wpq, keys)


PEER_EB = 512
PEER_RB = 16


def _gelu_tanh(x):
    return 0.5 * x * (1.0 + jnp.tanh(0.7978845608028654 * (x + 0.044715 * (x * x * x))))


def _peer_dense_kernel(ht_ref, a1_ref, n1_ref, r2_ref, e2_ref, u_ref, vt_ref, x1_ref, mod_ref, fw_ref,
                       y_ref, acc_ref, gt_ref, *, ne):
    e = pl.program_id(2)
    tt = ht_ref.shape[1]
    n_sub = PEER_EB // N_KEYS

    @pl.when(e == 0)
    def _():
        acc_ref[...] = jnp.zeros_like(acc_ref)

    def sub_body(sub, carry):
        i = e * n_sub + sub
        off = pl.multiple_of(sub * N_KEYS, N_KEYS)
        act = jnp.dot(u_ref[pl.ds(off, N_KEYS), :], ht_ref[...], preferred_element_type=F32)
        w = [None] * (N_KEYS // PEER_RB)
        for hd in range(PEER_HEADS):
            a1 = jnp.broadcast_to(a1_ref[hd, pl.ds(i, 1), :], (PEER_RB, tt)).astype(BF16)
            n1 = jnp.broadcast_to(n1_ref[hd, pl.ds(i, 1), :], (PEER_RB, tt)).astype(BF16)
            for rb in range(N_KEYS // PEER_RB):
                rows = slice(rb * PEER_RB, (rb + 1) * PEER_RB)
                wh = a1 * jnp.where(r2_ref[hd, rows, :] < n1, e2_ref[hd, rows, :], jnp.zeros((), BF16))
                w[rb] = wh if w[rb] is None else w[rb] + wh
        gt_ref[pl.ds(off, N_KEYS), :] = jnp.concatenate(w, axis=0) * _gelu_tanh(act).astype(BF16)
        return carry

    lax.fori_loop(0, n_sub, sub_body, 0)
    acc_ref[...] += jnp.dot(vt_ref[...], gt_ref[...], preferred_element_type=F32)

    @pl.when(e == ne - 1)
    def _():
        gate2 = mod_ref[:, 5 * D_MODEL:6 * D_MODEL]
        x2 = x1_ref[...] + gate2 * acc_ref[...].T
        y_ref[...] = x2 * lax.rsqrt(jnp.mean(x2 * x2, axis=-1, keepdims=True) + EPS) * fw_ref[...]


def _peer_dense(ht, a1, n1, r2, e2, u, vt, x1, mod, fw, tt):
    bm, t, d = x1.shape
    ne = u.shape[0] // PEER_EB
    sel = lambda: pl.BlockSpec((None, PEER_HEADS, N_KEYS, tt), lambda b, i, e: (b, 0, 0, i))
    return pl.pallas_call(
        functools.partial(_peer_dense_kernel, ne=ne),
        grid=(bm, t // tt, ne),
        in_specs=[pl.BlockSpec((None, d, tt), lambda b, i, e: (b, 0, i)),
                  sel(), sel(), sel(), sel(),
                  pl.BlockSpec((PEER_EB, d), lambda b, i, e: (e, 0)),
                  pl.BlockSpec((d, PEER_EB), lambda b, i, e: (0, e)),
                  pl.BlockSpec((None, tt, d), lambda b, i, e: (b, i, 0)),
                  pl.BlockSpec((None, 1, 6 * d), lambda b, i, e: (b, 0, 0)),
                  pl.BlockSpec((1, d), lambda b, i, e: (0, 0))],
        out_specs=pl.BlockSpec((None, tt, d), lambda b, i, e: (b, i, 0)),
        out_shape=jax.ShapeDtypeStruct((bm, t, d), F32),
        scratch_shapes=[pltpu.VMEM((d, tt), F32), pltpu.VMEM((PEER_EB, tt), BF16)],
        compiler_params=_cparams(("arbitrary", "arbitrary", "arbitrary"), 56),
        name="peer_dense",
    )(ht, a1, n1, r2, e2, u, vt, x1, mod, fw)


def _rope_tables(t):
    pos_t = np.arange(t)
    j = np.arange(HEAD_DIM)
    blk = j // (HEAD_DIM // 2)
    jj = j % (HEAD_DIM // 2)
    half = HEAD_DIM // 4
    inv = ROPE_THETA ** (-(jj % half).astype(np.float32) / half)
    pos = jnp.where(blk[None, :] == 0, (pos_t // GRID_W)[:, None], (pos_t % GRID_W)[:, None]).astype(F32)
    ang = pos * jnp.asarray(inv, F32)[None, :]
    cos, sin = jnp.cos(ang), jnp.sin(ang)
    first = jnp.asarray(jj < half)[None, :]
    sin_a = jnp.where(first, -sin, 0.0)
    sin_b = jnp.where(first, 0.0, sin)
    rep = lambda a: jnp.concatenate([a, a], axis=1)
    return rep(cos), rep(sin_a), rep(sin_b)


def _block_ones(n):
    idx = np.arange(n) // HEAD_DIM
    return jnp.asarray(idx[:, None] == idx[None, :], BF16)


def _group(x, mod, init, ctx_kv, seq_shape, wts, tables):
    bm, tmod, d = x.shape
    b, t = seq_shape
    seq = lambda a: a.reshape((b, t) + a.shape[2:])
    tok = lambda a: a.reshape((bm, tmod) + a.shape[2:])

    main, small = _inproj(x, mod, wts["norm1_w"], wts["w_cat"], 512)
    xbc = _conv(seq(main), wts["conv_w"], wts["conv_b"], 256)
    y2, fin = _ssd(xbc, seq(small), wts["ssd_prm"], init)
    qn, kn, vn = _prep(seq(main), seq(small), wts["qnw_t"], wts["knw_t"], wts["bq"], wts["bk"], tables, 256)
    if ctx_kv is None:
        keys, vals = kn, vn
    else:
        keys = jnp.concatenate([kn, ctx_kv[0]], axis=1)
        vals = jnp.concatenate([vn, ctx_kv[1]], axis=1)
    o = _attn(qn, keys, vals, 256)
    y2m = y2.reshape((2, bm, tmod, SSD_D_INNER))
    x1 = _merge(y2m, tok(xbc), main, tok(o), x, mod, wts["dsk"], wts["ssd_norm_w"],
                wts["w_ssd_out"], wts["w_attn_out"], wts["w_o"], 256)
    ht, a1, n1, r2, e2 = _peer_front(x1, mod, wts["norm2_w"], wts["w_pq"], wts["peer_keys"], 256)
    y = _peer_dense(ht, a1, n1, r2, e2, wts["peer_u"], wts["peer_vt"], x1, mod, wts["final_norm_w"], 512)
    return y, fin, kn, vn


def kernel(x_prompt, x_sample, cache_k, cache_v, state_ssm_fwd, state_ssm_bwd, c, c_ctx, w_ada, b_ada, norm1_w, norm2_w, w_in, conv_w, conv_b, a_log_f, a_log_b, dt_bias_f, dt_bias_b, d_skip, ssd_norm_w, w_ssd_out, q_norm_w, k_norm_w, w_attn_out, w_o, w_pq, peer_keys, peer_u, peer_v, final_norm_w):
    assert w_in.shape[0] == 1, "single-layer problem"
    bp, tp, d = x_prompt.shape
    bs, ts, _ = x_sample.shape

    wi = w_in[0]
    offs = np.cumsum([0, SSD_D_INNER, XBC_DIM, SSD_HEADS, SSD_HEADS, D_MODEL, KV_DIM, KV_DIM, D_MODEL, D_MODEL])
    part = lambda n: wi[:, offs[n]:offs[n + 1]]
    zpad = lambda n: jnp.zeros((d, n), F32)
    w_cat = jnp.concatenate(
        [part(0), part(1), part(4), part(7), part(8),
         part(5), part(6),
         part(2), zpad(LANES - SSD_HEADS), part(3), zpad(LANES - SSD_HEADS),
         zpad(SMALL_W - 2 * KV_DIM - 2 * LANES)], axis=1).astype(BF16)
    row = lambda a: a.reshape(1, -1).astype(F32)
    padl = lambda a: jnp.pad(a.astype(F32), (0, LANES - a.shape[0]))
    prm = jnp.zeros((2, 8, LANES), F32)
    prm = prm.at[0, 0].set(padl(a_log_f[0])).at[0, 1].set(padl(dt_bias_f[0]))
    prm = prm.at[1, 0].set(padl(a_log_b[0])).at[1, 1].set(padl(dt_bias_b[0]))
    wts = {
        "norm1_w": row(norm1_w[0]), "norm2_w": row(norm2_w[0]), "w_cat": w_cat,
        "conv_w": jnp.pad(conv_w[0], ((0, 8 - CONV_W), (0, 0))), "conv_b": row(conv_b[0]),
        "ssd_prm": prm,
        "dsk": row(jnp.repeat(d_skip[0], SSD_HEADDIM)), "ssd_norm_w": row(ssd_norm_w[0]),
        "w_ssd_out": w_ssd_out[0].astype(BF16), "w_attn_out": w_attn_out[0].astype(BF16),
        "w_o": w_o[0].astype(BF16),
        "qnw_t": row(jnp.tile(q_norm_w[0], N_HEADS)), "knw_t": row(jnp.tile(k_norm_w[0], N_KV_HEADS)),
        "bq": _block_ones(D_MODEL), "bk": _block_ones(KV_DIM),
        "w_pq": w_pq[0].astype(BF16),
        "peer_keys": peer_keys[0].reshape(PEER_HEADS * 2, N_KEYS, PEER_DQ // 2).astype(BF16),
        "peer_u": peer_u[0].astype(BF16), "peer_vt": peer_v[0].T.astype(BF16),
        "final_norm_w": row(final_norm_w),
    }

    cc = jnp.concatenate([c_ctx[None, :], c, jnp.zeros((8 - 1 - bs, d), F32)], axis=0)
    mod = _ada(cc, w_ada[0], row(b_ada[0]))
    mod_p = mod[0:1].reshape(1, 1, 6 * d)
    mod_s = mod[1:1 + bs].reshape(bs, 1, 6 * d)

    yp, fin_p, kn_p, vn_p = _group(x_prompt.reshape(1, bp * tp, d), mod_p, None, None, (bp, tp), wts, None)

    init = jnp.stack([state_ssm_fwd[:, 0], state_ssm_bwd[:, 0]], axis=0).reshape(2, bs, SSD_D_INNER, D_STATE)
    ctx_kv = (cache_k[:, 0].reshape(bs, -1, KV_DIM), cache_v[:, 0].reshape(bs, -1, KV_DIM))
    ys, _, _, _ = _group(x_sample, mod_s, init, ctx_kv, (bs, ts), wts, _rope_tables(ts))

    y_prompt = yp.reshape(bp, tp, d)
    new_cache_k = kn_p.reshape(bp, 1, tp, N_KV_HEADS, HEAD_DIM)
    new_cache_v = vn_p.reshape(bp, 1, tp, N_KV_HEADS, HEAD_DIM)
    new_state_fwd = fin_p[0].reshape(bp, 1, SSD_HEADS, SSD_HEADDIM, D_STATE)
    new_state_bwd = fin_p[1].reshape(bp, 1, SSD_HEADS, SSD_HEADDIM, D_STATE)
    return (y_prompt, ys, new_cache_k, new_cache_v, new_state_fwd, new_state_bwd)
```

```python
import functools

import jax
import jax.numpy as jnp
import numpy as np
from jax import lax
from jax.experimental import pallas as pl
from jax.experimental.pallas import tpu as pltpu

F32 = jnp.float32
BF16 = jnp.bfloat16

D_MODEL = 1024
GRID_W = 64
SSD_D_INNER = 2048
SSD_HEADDIM = 64
SSD_HEADS = 32
SSD_GROUPS = 8
D_STATE = 128
CONV_W = 5
CHUNK = 128
XBC_DIM = 4096
N_HEADS = 16
N_KV_HEADS = 4
HEAD_DIM = 64
KV_DIM = N_KV_HEADS * HEAD_DIM
ROPE_THETA = 10000.0
N_KEYS = 128
PEER_HEADS = 8
PEER_TOPK = 16
PEER_DQ = 256
EPS = 1e-6

LANES = 128
MIB = 1024 * 1024

IN_TN = 1024
COL_Z = 0
COL_XBC = 2
COL_Q = 6
COL_GA = 7
COL_GB = 8
COL_SMALL = 9
MAIN_W = 10 * IN_TN
SMALL_W = IN_TN
IN_MM_TN = 2048

NT_DIMS = (((1,), (1,)), ((), ()))


def _cparams(sem, vmem_mib):
    return pltpu.CompilerParams(dimension_semantics=sem, vmem_limit_bytes=vmem_mib * MIB)


def _sigmoid(x):
    return 1.0 / (1.0 + jnp.exp(-x))


def _silu(x):
    return x * _sigmoid(x)


def _softplus(x):
    return jnp.maximum(x, 0.0) + jnp.log(1.0 + jnp.exp(-jnp.abs(x)))


def _ada_kernel(c_ref, w_ref, b_ref, o_ref):
    sc = _silu(c_ref[...])
    o_ref[...] = jnp.dot(sc.astype(BF16), w_ref[...].astype(BF16),
                         preferred_element_type=F32) + b_ref[...]


def _ada(cc, w, b):
    rows, d = cc.shape
    n = w.shape[1]
    tn = 1536
    return pl.pallas_call(
        _ada_kernel,
        grid=(n // tn,),
        in_specs=[pl.BlockSpec((rows, d), lambda j: (0, 0)),
                  pl.BlockSpec((d, tn), lambda j: (0, j)),
                  pl.BlockSpec((1, tn), lambda j: (0, j))],
        out_specs=pl.BlockSpec((rows, tn), lambda j: (0, j)),
        out_shape=jax.ShapeDtypeStruct((rows, n), F32),
        compiler_params=_cparams(("arbitrary",), 32),
        name="ada",
    )(cc, w, b)


def _inproj_kernel(x_ref, mod_ref, nw_ref, w_ref, main_ref, small_ref, h_ref):
    j = pl.program_id(2)

    @pl.when(j == 0)
    def _():
        x = x_ref[...]
        r = lax.rsqrt(jnp.mean(x * x, axis=-1, keepdims=True) + EPS)
        shift = mod_ref[:, 0:D_MODEL]
        scale = mod_ref[:, D_MODEL:2 * D_MODEL]
        h_ref[...] = ((x * r * nw_ref[...]) * (1.0 + scale) + shift).astype(BF16)

    acc = jnp.dot(h_ref[...], w_ref[...], preferred_element_type=F32)
    main_ref[...] = acc.astype(BF16)

    @pl.when(j == MAIN_W // IN_MM_TN - 1)
    def _():
        lo = COL_SMALL * IN_TN - (MAIN_W - IN_MM_TN)
        small_ref[...] = acc[:, lo:lo + SMALL_W]


def _inproj(x, mod, nw, w_cat, tm):
    bm, t, d = x.shape
    return pl.pallas_call(
        _inproj_kernel,
        grid=(bm, t // tm, MAIN_W // IN_MM_TN),
        in_specs=[pl.BlockSpec((None, tm, d), lambda b, i, j: (b, i, 0)),
                  pl.BlockSpec((None, 1, 6 * d), lambda b, i, j: (b, 0, 0)),
                  pl.BlockSpec((1, d), lambda b, i, j: (0, 0)),
                  pl.BlockSpec((d, IN_MM_TN), lambda b, i, j: (0, j))],
        out_specs=[pl.BlockSpec((None, tm, IN_MM_TN), lambda b, i, j: (b, i, j)),
                   pl.BlockSpec((None, tm, SMALL_W), lambda b, i, j: (b, i, 0))],
        out_shape=[jax.ShapeDtypeStruct((bm, t, MAIN_W), BF16),
                   jax.ShapeDtypeStruct((bm, t, SMALL_W), F32)],
        scratch_shapes=[pltpu.VMEM((tm, d), BF16)],
        compiler_params=_cparams(("arbitrary", "arbitrary", "arbitrary"), 40),
        name="inproj",
    )(x, mod, nw, w_cat)


CONV_HALO = 16


def _conv_kernel(cur_ref, prev_ref, next_ref, w_ref, b_ref, o_ref, *, nt, tt):
    i = pl.program_id(1)
    cur = cur_ref[...].astype(F32)
    pv = prev_ref[...].astype(F32)[8:16]
    nx = next_ref[...].astype(F32)[0:8]
    pv = jnp.where(i > 0, pv, 0.0)
    nx = jnp.where(i < nt - 1, nx, 0.0)
    ext = jnp.concatenate([pv, cur, nx], axis=0)
    acc = jnp.broadcast_to(b_ref[...], cur.shape)
    for k in range(CONV_W):
        off = 8 + k - CONV_W // 2
        acc = acc + w_ref[k:k + 1, :] * ext[off:off + tt]
    o_ref[...] = _silu(acc).astype(BF16)


def _conv(main, conv_w, conv_b, tt):
    b, t, _ = main.shape
    tc = IN_TN
    nt = t // tt
    hb = tt // CONV_HALO
    last = t // CONV_HALO - 1
    return pl.pallas_call(
        functools.partial(_conv_kernel, nt=nt, tt=tt),
        grid=(b, nt, XBC_DIM // tc),
        in_specs=[pl.BlockSpec((None, tt, tc), lambda bb, i, j: (bb, i, COL_XBC + j)),
                  pl.BlockSpec((None, CONV_HALO, tc),
                               lambda bb, i, j: (bb, jnp.maximum(i * hb - 1, 0), COL_XBC + j)),
                  pl.BlockSpec((None, CONV_HALO, tc),
                               lambda bb, i, j: (bb, jnp.minimum((i + 1) * hb, last), COL_XBC + j)),
                  pl.BlockSpec((8, tc), lambda bb, i, j: (0, j)),
                  pl.BlockSpec((1, tc), lambda bb, i, j: (0, j))],
        out_specs=pl.BlockSpec((None, tt, tc), lambda bb, i, j: (bb, i, j)),
        out_shape=jax.ShapeDtypeStruct((b, t, XBC_DIM), BF16),
        compiler_params=_cparams(("arbitrary", "arbitrary", "arbitrary"), 32),
        name="conv",
    )(main, main, main, conv_w, conv_b)


N_PAIRS = SSD_HEADS // 2


def _ssd_kernel(*refs, nc, has_init):
    if has_init:
        xs_ref, b_ref, c_ref, dt_ref, prm_ref, init_ref, y_ref, fin_ref, st_ref = refs
    else:
        xs_ref, b_ref, c_ref, dt_ref, prm_ref, y_ref, fin_ref, st_ref = refs
    d = pl.program_id(0)
    c = pl.program_id(2)

    @pl.when(c == 0)
    def _():
        for pp in range(N_PAIRS):
            if has_init:
                st_ref[pp] = init_ref[pp * LANES:(pp + 1) * LANES, :].T
            else:
                st_ref[pp] = jnp.zeros((D_STATE, LANES), F32)

    row = lax.broadcasted_iota(jnp.int32, (CHUNK, CHUNK), 0)
    col = lax.broadcasted_iota(jnp.int32, (CHUNK, CHUNK), 1)
    tri = (row - col) * (1 - 2 * d) >= 0
    lo = col < SSD_HEADDIM

    a_log = prm_ref[0:1, :]
    bias = prm_ref[1:2, :]
    dt = _softplus(dt_ref[...] + bias)
    a = dt * (-jnp.exp(a_log))
    cs = jnp.dot(tri.astype(F32), a, precision=lax.Precision.HIGHEST,
                 preferred_element_type=F32)
    tot = jnp.sum(a, axis=0, keepdims=True)
    w_t = (jnp.exp(tot - cs) * dt).T
    r_t = (cs - jnp.log(dt)).T
    dec = jnp.exp(tot)

    for g in range(SSD_GROUPS):
        bg = b_ref[:, g * D_STATE:(g + 1) * D_STATE]
        cg = c_ref[:, g * D_STATE:(g + 1) * D_STATE]
        sc = lax.dot_general(cg, bg, NT_DIMS, preferred_element_type=F32)
        bg_t = bg.astype(F32).T
        cgf = cg.astype(F32)
        for q in range(2):
            pp = g * 2 + q
            xp = xs_ref[:, pp * LANES:(pp + 1) * LANES]
            st_old = st_ref[pp]
            rhs = jnp.concatenate([xp, st_old.astype(BF16)], axis=0)
            ys, us = [], []
            for e in range(2):
                h = pp * 2 + e
                csb = jnp.broadcast_to(cs[:, h:h + 1], (CHUNK, CHUNK))
                m = sc * jnp.exp(jnp.where(tri, csb - r_t[h:h + 1, :], -jnp.inf))
                cse = cgf * jnp.exp(csb)
                lhs = jnp.concatenate([m.astype(BF16), cse.astype(BF16)], axis=1)
                ys.append(jnp.dot(lhs, rhs, preferred_element_type=F32))
                bw = (bg_t * w_t[h:h + 1, :]).astype(BF16)
                us.append(jnp.dot(bw, xp, preferred_element_type=F32))
            y_ref[:, pp * LANES:(pp + 1) * LANES] = jnp.where(lo, ys[0], ys[1]).astype(BF16)
            decrow = jnp.where(lo[0:1, :], dec[:, 2 * pp:2 * pp + 1], dec[:, 2 * pp + 1:2 * pp + 2])
            st_ref[pp] = st_old * decrow + jnp.where(lo, us[0], us[1])

    @pl.when(c == nc - 1)
    def _():
        for pp in range(N_PAIRS):
            fin_ref[pp * LANES:(pp + 1) * LANES, :] = st_ref[pp].T


def _ssd(xbc, small, prm, init):
    b, t, _ = xbc.shape
    nc = t // CHUNK
    has_init = init is not None

    def ce(d, c):
        return c + d * (nc - 1 - 2 * c)

    in_specs = [pl.BlockSpec((None, CHUNK, SSD_D_INNER), lambda d, bb, c: (bb, ce(d, c), 0)),
                pl.BlockSpec((None, CHUNK, SSD_GROUPS * D_STATE), lambda d, bb, c: (bb, ce(d, c), 2)),
                pl.BlockSpec((None, CHUNK, SSD_GROUPS * D_STATE), lambda d, bb, c: (bb, ce(d, c), 3)),
                pl.BlockSpec((None, CHUNK, LANES), lambda d, bb, c: (bb, ce(d, c), 4 + d)),
                pl.BlockSpec((None, 8, LANES), lambda d, bb, c: (d, 0, 0))]
    args = [xbc, xbc, xbc, small, prm]
    if has_init:
        in_specs.append(pl.BlockSpec((None, None, SSD_D_INNER, D_STATE), lambda d, bb, c: (d, bb, 0, 0)))
        args.append(init)
    return pl.pallas_call(
        functools.partial(_ssd_kernel, nc=nc, has_init=has_init),
        grid=(2, b, nc),
        in_specs=in_specs,
        out_specs=[pl.BlockSpec((None, None, CHUNK, SSD_D_INNER), lambda d, bb, c: (d, bb, ce(d, c), 0)),
                   pl.BlockSpec((None, None, SSD_D_INNER, D_STATE), lambda d, bb, c: (d, bb, 0, 0))],
        out_shape=[jax.ShapeDtypeStruct((2, b, t, SSD_D_INNER), BF16),
                   jax.ShapeDtypeStruct((2, b, SSD_D_INNER, D_STATE), F32)],
        scratch_shapes=[pltpu.VMEM((N_PAIRS, D_STATE, LANES), F32)],
        compiler_params=_cparams(("arbitrary", "arbitrary", "arbitrary"), 40),
        name="ssd",
    )(*args)


def _rope(x, cos, sin_a, sin_b, reps):
    n = x.shape[1]
    half = HEAD_DIM // 4
    cosf = jnp.concatenate([cos] * reps, axis=1)
    saf = jnp.concatenate([sin_a] * reps, axis=1)
    sbf = jnp.concatenate([sin_b] * reps, axis=1)
    return x * cosf + pltpu.roll(x, n - half, 1) * saf + pltpu.roll(x, half, 1) * sbf


def _prep_kernel(*refs, rope):
    if rope:
        q_ref, kv_ref, qnw_ref, knw_ref, bq_ref, bk_ref, cos_ref, sa_ref, sb_ref, qo_ref, ko_ref, vo_ref = refs
    else:
        q_ref, kv_ref, qnw_ref, knw_ref, bq_ref, bk_ref, qo_ref, ko_ref, vo_ref = refs
    q = q_ref[...].astype(F32)
    k = kv_ref[:, 0:KV_DIM]
    vo_ref[...] = kv_ref[:, KV_DIM:2 * KV_DIM]
    qss = jnp.dot((q * q).astype(BF16), bq_ref[...], preferred_element_type=F32)
    kss = jnp.dot((k * k).astype(BF16), bk_ref[...], preferred_element_type=F32)
    qn = q * lax.rsqrt(qss * (1.0 / HEAD_DIM) + EPS) * qnw_ref[...]
    kn = k * lax.rsqrt(kss * (1.0 / HEAD_DIM) + EPS) * knw_ref[...]
    if rope:
        cos, sa, sb = cos_ref[...], sa_ref[...], sb_ref[...]
        qn = _rope(qn, cos, sa, sb, D_MODEL // LANES)
        kn = _rope(kn, cos, sa, sb, KV_DIM // LANES)
    qo_ref[...] = (qn * (HEAD_DIM ** -0.5)).astype(BF16)
    ko_ref[...] = kn


def _prep(main, small, qnw_t, knw_t, bq, bk, tables, tt):
    b, t, _ = main.shape
    rope = tables is not None
    in_specs = [pl.BlockSpec((None, tt, D_MODEL), lambda bb, i: (bb, i, COL_Q)),
                pl.BlockSpec((None, tt, 2 * KV_DIM), lambda bb, i: (bb, i, 0)),
                pl.BlockSpec((1, D_MODEL), lambda bb, i: (0, 0)),
                pl.BlockSpec((1, KV_DIM), lambda bb, i: (0, 0)),
                pl.BlockSpec((D_MODEL, D_MODEL), lambda bb, i: (0, 0)),
                pl.BlockSpec((KV_DIM, KV_DIM), lambda bb, i: (0, 0))]
    args = [main, small, qnw_t, knw_t, bq, bk]
    if rope:
        in_specs += [pl.BlockSpec((tt, LANES), lambda bb, i: (i, 0))] * 3
        args += list(tables)
    return pl.pallas_call(
        functools.partial(_prep_kernel, rope=rope),
        grid=(b, t // tt),
        in_specs=in_specs,
        out_specs=[pl.BlockSpec((None, tt, D_MODEL), lambda bb, i: (bb, i, 0)),
                   pl.BlockSpec((None, tt, KV_DIM), lambda bb, i: (bb, i, 0)),
                   pl.BlockSpec((None, tt, KV_DIM), lambda bb, i: (bb, i, 0))],
        out_shape=[jax.ShapeDtypeStruct((b, t, D_MODEL), BF16),
                   jax.ShapeDtypeStruct((b, t, KV_DIM), F32),
                   jax.ShapeDtypeStruct((b, t, KV_DIM), F32)],
        compiler_params=_cparams(("arbitrary", "arbitrary"), 32),
        name="qk_prep",
    )(*args)


def _attn_kernel(q_ref, k_ref, v_ref, o_ref, kx_ref, vx_ref):
    qi = pl.program_id(1)
    tk = k_ref.shape[0]

    @pl.when(qi == 0)
    def _():
        lo = lax.broadcasted_iota(jnp.int32, (tk, LANES), 1) < HEAD_DIM
        for pr in range(KV_DIM // LANES):
            for src, dst in ((k_ref, kx_ref), (v_ref, vx_ref)):
                a = src[:, pr * LANES:(pr + 1) * LANES].astype(F32)
                ar = pltpu.roll(a, HEAD_DIM, 1)
                dst[4 * pr + 0] = jnp.where(lo, a, 0.0).astype(BF16)
                dst[4 * pr + 1] = jnp.where(lo, 0.0, ar).astype(BF16)
                dst[4 * pr + 2] = jnp.where(lo, ar, 0.0).astype(BF16)
                dst[4 * pr + 3] = jnp.where(lo, 0.0, a).astype(BF16)

    for pp in range(N_HEADS // 2):
        g = pp // 2
        qp = q_ref[:, pp * LANES:(pp + 1) * LANES]
        o = None
        for e in range(2):
            s = lax.dot_general(qp, kx_ref[2 * g + e], NT_DIMS, preferred_element_type=F32)
            m = jnp.max(s, axis=-1, keepdims=True)
            p = jnp.exp(s - m)
            l = jnp.sum(p, axis=-1, keepdims=True)
            oe = jnp.dot(p.astype(BF16), vx_ref[2 * g + e], preferred_element_type=F32) / l
            o = oe if o is None else o + oe
        o_ref[:, pp * LANES:(pp + 1) * LANES] = o.astype(BF16)


def _attn(q, k, v, tq):
    b, t, _ = q.shape
    tk = k.shape[1]
    return pl.pallas_call(
        _attn_kernel,
        grid=(b, t // tq),
        in_specs=[pl.BlockSpec((None, tq, D_MODEL), lambda bb, i: (bb, i, 0)),
                  pl.BlockSpec((None, tk, KV_DIM), lambda bb, i: (bb, 0, 0)),
                  pl.BlockSpec((None, tk, KV_DIM), lambda bb, i: (bb, 0, 0))],
        out_specs=pl.BlockSpec((None, tq, D_MODEL), lambda bb, i: (bb, i, 0)),
        out_shape=jax.ShapeDtypeStruct((b, t, D_MODEL), BF16),
        scratch_shapes=[pltpu.VMEM((2 * N_KV_HEADS, tk, LANES), BF16),
                        pltpu.VMEM((2 * N_KV_HEADS, tk, LANES), BF16)],
        compiler_params=_cparams(("arbitrary", "arbitrary"), 48),
        name="attn",
    )(q, k, v)


def _merge_kernel(yf_ref, yb_ref, xs_ref, z_ref, o_ref, ga_ref, gb_ref, x_ref, mod_ref,
                  dsk_ref, snw_ref, wso_ref, wao_ref, wo_ref, x1_ref):
    y = yf_ref[...].astype(F32) + yb_ref[...].astype(F32) + dsk_ref[...] * xs_ref[...].astype(F32)
    y = y * _silu(z_ref[...].astype(F32))
    y = y * lax.rsqrt(jnp.mean(y * y, axis=-1, keepdims=True) + EPS) * snw_ref[...]
    ya = jnp.dot(y.astype(BF16), wso_ref[...], preferred_element_type=F32)
    yb2 = jnp.dot(o_ref[...], wao_ref[...], preferred_element_type=F32)
    merged = _sigmoid(ga_ref[...].astype(F32)) * ya + _sigmoid(gb_ref[...].astype(F32)) * yb2
    mix = jnp.dot(merged.astype(BF16), wo_ref[...], preferred_element_type=F32)
    gate1 = mod_ref[:, 2 * D_MODEL:3 * D_MODEL]
    x1_ref[...] = x_ref[...] + gate1 * mix


def _merge(y2, xbc, main, o, x, mod, dsk, snw, wso, wao, wo, tm):
    bm, t, d = x.shape
    tok = lambda w, cb: pl.BlockSpec((None, tm, w), lambda b, i: (b, i, cb))
    full = lambda a: pl.BlockSpec(a.shape, lambda b, i: (0,) * a.ndim)
    return pl.pallas_call(
        _merge_kernel,
        grid=(bm, t // tm),
        in_specs=[pl.BlockSpec((None, None, tm, SSD_D_INNER), lambda b, i: (0, b, i, 0)),
                  pl.BlockSpec((None, None, tm, SSD_D_INNER), lambda b, i: (1, b, i, 0)),
                  tok(SSD_D_INNER, 0), tok(SSD_D_INNER, COL_Z), tok(d, 0),
                  tok(d, COL_GA), tok(d, COL_GB), tok(d, 0),
                  pl.BlockSpec((None, 1, 6 * d), lambda b, i: (b, 0, 0)),
                  full(dsk), full(snw), full(wso), full(wao), full(wo)],
        out_specs=tok(d, 0),
        out_shape=jax.ShapeDtypeStruct((bm, t, d), F32),
        compiler_params=_cparams(("arbitrary", "arbitrary"), 48),
        name="merge",
    )(y2, y2, xbc, main, o, main, main, x, mod, dsk, snw, wso, wao, wo)


def _top_rows(s, k, ranks=False):
    rows = []
    rank = jnp.full(s.shape, float(k), F32) if ranks else None
    for it in range(k):
        m = jnp.max(s, axis=0, keepdims=True)
        rows.append(m)
        hit = s == m
        s = jnp.where(hit, -jnp.inf, s)
        if ranks:
            rank = jnp.where(hit, float(it), rank)
    return (rows, rank) if ranks else rows


def _peer_front_kernel(x_ref, mod_ref, nw_ref, wpq_ref, keys_ref, ht_ref, a1_ref, n1_ref, r2_ref, e2_ref):
    x = x_ref[...]
    r = lax.rsqrt(jnp.mean(x * x, axis=-1, keepdims=True) + EPS)
    shift = mod_ref[:, 3 * D_MODEL:4 * D_MODEL]
    scale = mod_ref[:, 4 * D_MODEL:5 * D_MODEL]
    h = (x * r * nw_ref[...]) * (1.0 + scale) + shift
    hb = h.astype(BF16)
    ht_ref[...] = h.T.astype(BF16)
    qp = jnp.dot(hb, wpq_ref[...], preferred_element_type=F32).astype(BF16)
    k = PEER_TOPK
    for hd in range(PEER_HEADS):
        s1 = lax.dot_general(keys_ref[2 * hd], qp[:, (2 * hd) * LANES:(2 * hd + 1) * LANES],
                             NT_DIMS, preferred_element_type=F32)
        s2 = lax.dot_general(keys_ref[2 * hd + 1], qp[:, (2 * hd + 1) * LANES:(2 * hd + 2) * LANES],
                             NT_DIMS, preferred_element_type=F32)
        ta = _top_rows(s1, k)
        tb_rows, rank2 = _top_rows(s2, k, ranks=True)
        tb = jnp.concatenate(tb_rows, axis=0)
        pieces = [ta[0] + tb] + [ta[p] + tb[0:8] for p in range(1, 8)]
        pieces.append(jnp.concatenate(ta[8:], axis=0) + tb[0:1])
        best = _top_rows(jnp.concatenate(pieces, axis=0), k)
        zsum = jnp.ones_like(best[0])
        for v in best[1:]:
            zsum = zsum + jnp.exp(v - best[0])
        thr = best[-1]
        ta_arr = jnp.concatenate(ta, axis=0)
        n1 = jnp.zeros_like(s1)
        for q in range(k):
            theta = jnp.min(jnp.where(ta_arr + tb_rows[q] >= thr, ta_arr, jnp.inf), axis=0, keepdims=True)
            n1 = jnp.where(s1 >= theta, float(q + 1), n1)
        a1_ref[hd] = jnp.exp(s1 - ta[0])
        n1_ref[hd] = n1
        r2_ref[hd] = rank2.astype(BF16)
        e2_ref[hd] = (jnp.exp(s2 - tb_rows[0]) / zsum).astype(BF16)


def _peer_front(x1, mod, nw, wpq, keys, tm):
    bm, t, d = x1.shape
    sel = lambda: pl.BlockSpec((None, PEER_HEADS, N_KEYS, tm), lambda b, i: (b, 0, 0, i))
    sel_shape = lambda dt: jax.ShapeDtypeStruct((bm, PEER_HEADS, N_KEYS, t), dt)
    return pl.pallas_call(
        _peer_front_kernel,
        grid=(bm, t // tm),
        in_specs=[pl.BlockSpec((None, tm, d), lambda b, i: (b, i, 0)),
                  pl.BlockSpec((None, 1, 6 * d), lambda b, i: (b, 0, 0)),
                  pl.BlockSpec((1, d), lambda b, i: (0, 0)),
                  pl.BlockSpec(wpq.shape, lambda b, i: (0, 0)),
                  pl.BlockSpec(keys.shape, lambda b, i: (0, 0, 0))],
        out_specs=[pl.BlockSpec((None, d, tm), lambda b, i: (b, 0, i)), sel(), sel(), sel(), sel()],
        out_shape=[jax.ShapeDtypeStruct((bm, d, t), BF16), sel_shape(F32), sel_shape(F32),
                   sel_shape(BF16), sel_shape(BF16)],
        compiler_params=_cparams(("arbitrary", "arbitrary"), 48),
        name="peer_front",
    )(x1, mod, nw, wpq, keys)


PEER_EB = 1024
PEER_RB = 16
PEER_LW = 256
PEER_VK = 4


def _gelu_tanh(x):
    return 0.5 * x * (1.0 + jnp.tanh(0.7978845608028654 * (x + 0.044715 * (x * x * x))))


def _peer_dense_kernel(ht_ref, a1_ref, n1_ref, r2_ref, e2_ref, u_ref, v_ref, x1_ref, mod_ref, fw_ref,
                       y_ref, acc_ref, gt_ref, *, ne):
    e = pl.program_id(2)
    tt = ht_ref.shape[1]
    n_sub = PEER_EB // N_KEYS

    @pl.when(e == 0)
    def _():
        acc_ref[...] = jnp.zeros_like(acc_ref)

    n_rb = N_KEYS // PEER_RB

    def activations(sub):
        act = jnp.dot(u_ref[sub * N_KEYS:(sub + 1) * N_KEYS, :], ht_ref[...], preferred_element_type=F32)
        return _gelu_tanh(act.astype(BF16))

    def v_matmul(sub_hi):
        ch = slice((sub_hi + 1 - PEER_VK) * N_KEYS, (sub_hi + 1) * N_KEYS)
        acc_ref[...] += lax.dot_general(v_ref[ch, :], gt_ref[ch, :], (((0,), (0,)), ((), ())),
                                        preferred_element_type=F32)

    g_next = activations(0)
    for sub in range(n_sub):
        g = g_next
        if sub + 1 < n_sub:
            g_next = activations(sub + 1)
        if sub >= PEER_VK and sub % PEER_VK == 0:
            v_matmul(sub - 1)
        i = e * n_sub + sub
        a1_rows = [a1_ref[hd, pl.ds(i, 1), :] for hd in range(PEER_HEADS)]
        n1_rows = [n1_ref[hd, pl.ds(i, 1), :] for hd in range(PEER_HEADS)]
        for lg in range(tt // PEER_LW):
            ls = slice(lg * PEER_LW, (lg + 1) * PEER_LW)
            w = [None] * n_rb
            for hd in range(PEER_HEADS):
                a1 = jnp.broadcast_to(a1_rows[hd][:, ls], (PEER_RB, PEER_LW)).astype(BF16)
                n1 = jnp.broadcast_to(n1_rows[hd][:, ls], (PEER_RB, PEER_LW)).astype(BF16)
                for rb in range(n_rb):
                    rows = slice(rb * PEER_RB, (rb + 1) * PEER_RB)
                    wh = a1 * jnp.where(r2_ref[hd, rows, ls] < n1, e2_ref[hd, rows, ls], jnp.zeros((), BF16))
                    w[rb] = wh if w[rb] is None else w[rb] + wh
            for rb in range(n_rb):
                rows = slice(rb * PEER_RB, (rb + 1) * PEER_RB)
                gt_ref[sub * N_KEYS + rb * PEER_RB:sub * N_KEYS + (rb + 1) * PEER_RB, ls] = w[rb] * g[rows, ls]
    v_matmul(n_sub - 1)

    @pl.when(e == ne - 1)
    def _():
        gate2 = mod_ref[:, 5 * D_MODEL:6 * D_MODEL]
        x2 = x1_ref[...] + gate2 * acc_ref[...].T
        y_ref[...] = x2 * lax.rsqrt(jnp.mean(x2 * x2, axis=-1, keepdims=True) + EPS) * fw_ref[...]


def _peer_dense(ht, a1, n1, r2, e2, u, v, x1, mod, fw, tt):
    bm, t, d = x1.shape
    ne = u.shape[0] // PEER_EB
    sel = lambda: pl.BlockSpec((None, PEER_HEADS, N_KEYS, tt), lambda b, i, e: (b, 0, 0, i))
    return pl.pallas_call(
        functools.partial(_peer_dense_kernel, ne=ne),
        grid=(bm, t // tt, ne),
        in_specs=[pl.BlockSpec((None, d, tt), lambda b, i, e: (b, 0, i)),
                  sel(), sel(), sel(), sel(),
                  pl.BlockSpec((PEER_EB, d), lambda b, i, e: (e, 0)),
                  pl.BlockSpec((PEER_EB, d), lambda b, i, e: (e, 0)),
                  pl.BlockSpec((None, tt, d), lambda b, i, e: (b, i, 0)),
                  pl.BlockSpec((None, 1, 6 * d), lambda b, i, e: (b, 0, 0)),
                  pl.BlockSpec((1, d), lambda b, i, e: (0, 0))],
        out_specs=pl.BlockSpec((None, tt, d), lambda b, i, e: (b, i, 0)),
        out_shape=jax.ShapeDtypeStruct((bm, t, d), F32),
        scratch_shapes=[pltpu.VMEM((d, tt), F32), pltpu.VMEM((PEER_EB, tt), BF16)],
        compiler_params=_cparams(("arbitrary", "arbitrary", "arbitrary"), 56),
        name="peer_dense",
    )(ht, a1, n1, r2, e2, u, v, x1, mod, fw)


def _rope_tables(t):
    pos_t = np.arange(t)
    j = np.arange(HEAD_DIM)
    blk = j // (HEAD_DIM // 2)
    jj = j % (HEAD_DIM // 2)
    half = HEAD_DIM // 4
    inv = ROPE_THETA ** (-(jj % half).astype(np.float32) / half)
    pos = jnp.where(blk[None, :] == 0, (pos_t // GRID_W)[:, None], (pos_t % GRID_W)[:, None]).astype(F32)
    ang = pos * jnp.asarray(inv, F32)[None, :]
    cos, sin = jnp.cos(ang), jnp.sin(ang)
    first = jnp.asarray(jj < half)[None, :]
    sin_a = jnp.where(first, -sin, 0.0)
    sin_b = jnp.where(first, 0.0, sin)
    rep = lambda a: jnp.concatenate([a, a], axis=1)
    return rep(cos), rep(sin_a), rep(sin_b)


def _block_ones(n):
    idx = np.arange(n) // HEAD_DIM
    return jnp.asarray(idx[:, None] == idx[None, :], BF16)


def _group(x, mod, init, ctx_kv, seq_shape, wts, tables):
    bm, tmod, d = x.shape
    b, t = seq_shape
    seq = lambda a: a.reshape((b, t) + a.shape[2:])
    tok = lambda a: a.reshape((bm, tmod) + a.shape[2:])

    main, small = _inproj(x, mod, wts["norm1_w"], wts["w_cat"], 512)
    xbc = _conv(seq(main), wts["conv_w"], wts["conv_b"], 256)
    y2, fin = _ssd(xbc, seq(small), wts["ssd_prm"], init)
    qn, kn, vn = _prep(seq(main), seq(small), wts["qnw_t"], wts["knw_t"], wts["bq"], wts["bk"], tables, 256)
    if ctx_kv is None:
        keys, vals = kn, vn
    else:
        keys = jnp.concatenate([kn, ctx_kv[0]], axis=1)
        vals = jnp.concatenate([vn, ctx_kv[1]], axis=1)
    o = _attn(qn, keys, vals, 256)
    y2m = y2.reshape((2, bm, tmod, SSD_D_INNER))
    x1 = _merge(y2m, tok(xbc), main, tok(o), x, mod, wts["dsk"], wts["ssd_norm_w"],
                wts["w_ssd_out"], wts["w_attn_out"], wts["w_o"], 256)
    ht, a1, n1, r2, e2 = _peer_front(x1, mod, wts["norm2_w"], wts["w_pq"], wts["peer_keys"], 256)
    y = _peer_dense(ht, a1, n1, r2, e2, wts["peer_u"], wts["peer_v"], x1, mod, wts["final_norm_w"], 512)
    return y, fin, kn, vn


def kernel(x_prompt, x_sample, cache_k, cache_v, state_ssm_fwd, state_ssm_bwd, c, c_ctx, w_ada, b_ada, norm1_w, norm2_w, w_in, conv_w, conv_b, a_log_f, a_log_b, dt_bias_f, dt_bias_b, d_skip, ssd_norm_w, w_ssd_out, q_norm_w, k_norm_w, w_attn_out, w_o, w_pq, peer_keys, peer_u, peer_v, final_norm_w):
    assert w_in.shape[0] == 1, "single-layer problem"
    bp, tp, d = x_prompt.shape
    bs, ts, _ = x_sample.shape

    wi = w_in[0]
    offs = np.cumsum([0, SSD_D_INNER, XBC_DIM, SSD_HEADS, SSD_HEADS, D_MODEL, KV_DIM, KV_DIM, D_MODEL, D_MODEL])
    part = lambda n: wi[:, offs[n]:offs[n + 1]]
    zpad = lambda n: jnp.zeros((d, n), F32)
    w_cat = jnp.concatenate(
        [part(0), part(1), part(4), part(7), part(8),
         part(5), part(6),
         part(2), zpad(LANES - SSD_HEADS), part(3), zpad(LANES - SSD_HEADS),
         zpad(SMALL_W - 2 * KV_DIM - 2 * LANES)], axis=1).astype(BF16)
    row = lambda a: a.reshape(1, -1).astype(F32)
    padl = lambda a: jnp.pad(a.astype(F32), (0, LANES - a.shape[0]))
    prm = jnp.zeros((2, 8, LANES), F32)
    prm = prm.at[0, 0].set(padl(a_log_f[0])).at[0, 1].set(padl(dt_bias_f[0]))
    prm = prm.at[1, 0].set(padl(a_log_b[0])).at[1, 1].set(padl(dt_bias_b[0]))
    wts = {
        "norm1_w": row(norm1_w[0]), "norm2_w": row(norm2_w[0]), "w_cat": w_cat,
        "conv_w": jnp.pad(conv_w[0], ((0, 8 - CONV_W), (0, 0))), "conv_b": row(conv_b[0]),
        "ssd_prm": prm,
        "dsk": row(jnp.repeat(d_skip[0], SSD_HEADDIM)), "ssd_norm_w": row(ssd_norm_w[0]),
        "w_ssd_out": w_ssd_out[0].astype(BF16), "w_attn_out": w_attn_out[0].astype(BF16),
        "w_o": w_o[0].astype(BF16),
        "qnw_t": row(jnp.tile(q_norm_w[0], N_HEADS)), "knw_t": row(jnp.tile(k_norm_w[0], N_KV_HEADS)),
        "bq": _block_ones(D_MODEL), "bk": _block_ones(KV_DIM),
        "w_pq": w_pq[0].astype(BF16),
        "peer_keys": peer_keys[0].reshape(PEER_HEADS * 2, N_KEYS, PEER_DQ // 2).astype(BF16),
        "peer_u": peer_u[0].astype(BF16), "peer_v": peer_v[0].astype(BF16),
        "final_norm_w": row(final_norm_w),
    }

    cc = jnp.concatenate([c_ctx[None, :], c, jnp.zeros((8 - 1 - bs, d), F32)], axis=0)
    mod = _ada(cc, w_ada[0], row(b_ada[0]))
    mod_p = mod[0:1].reshape(1, 1, 6 * d)
    mod_s = mod[1:1 + bs].reshape(bs, 1, 6 * d)

    yp, fin_p, kn_p, vn_p = _group(x_prompt.reshape(1, bp * tp, d), mod_p, None, None, (bp, tp), wts, None)

    init = jnp.stack([state_ssm_fwd[:, 0], state_ssm_bwd[:, 0]], axis=0).reshape(2, bs, SSD_D_INNER, D_STATE)
    ctx_kv = (cache_k[:, 0].reshape(bs, -1, KV_DIM), cache_v[:, 0].reshape(bs, -1, KV_DIM))
    ys, _, _, _ = _group(x_sample, mod_s, init, ctx_kv, (bs, ts), wts, _rope_tables(ts))

    y_prompt = yp.reshape(bp, tp, d)
    new_cache_k = kn_p.reshape(bp, 1, tp, N_KV_HEADS, HEAD_DIM)
    new_cache_v = vn_p.reshape(bp, 1, tp, N_KV_HEADS, HEAD_DIM)
    new_state_fwd = fin_p[0].reshape(bp, 1, SSD_HEADS, SSD_HEADDIM, D_STATE)
    new_state_bwd = fin_p[1].reshape(bp, 1, SSD_HEADS, SSD_HEADDIM, D_STATE)
    return (y_prompt, ys, new_cache_k, new_cache_v, new_state_fwd, new_state_bwd)
```

```python
import functools

import jax
import jax.numpy as jnp
import numpy as np
from jax import lax
from jax.experimental import pallas as pl
from jax.experimental.pallas import tpu as pltpu

F32 = jnp.float32
BF16 = jnp.bfloat16

D_MODEL = 1024
GRID_W = 64
SSD_D_INNER = 2048
SSD_HEADDIM = 64
SSD_HEADS = 32
SSD_GROUPS = 8
D_STATE = 128
CONV_W = 5
CHUNK = 128
XBC_DIM = 4096
N_HEADS = 16
N_KV_HEADS = 4
HEAD_DIM = 64
KV_DIM = N_KV_HEADS * HEAD_DIM
ROPE_THETA = 10000.0
N_KEYS = 128
PEER_HEADS = 8
PEER_TOPK = 16
PEER_DQ = 256
EPS = 1e-6

LANES = 128
MIB = 1024 * 1024

IN_TN = 1024
COL_Z = 0
COL_XBC = 2
COL_Q = 6
COL_GA = 7
COL_GB = 8
COL_SMALL = 9
MAIN_W = 10 * IN_TN
SMALL_W = IN_TN
IN_MM_TN = 2048

NT_DIMS = (((1,), (1,)), ((), ()))


def _cparams(sem, vmem_mib):
    return pltpu.CompilerParams(dimension_semantics=sem, vmem_limit_bytes=vmem_mib * MIB)


def _sigmoid(x):
    return 1.0 / (1.0 + jnp.exp(-x))


def _silu(x):
    return x * _sigmoid(x)


def _softplus(x):
    return jnp.maximum(x, 0.0) + jnp.log(1.0 + jnp.exp(-jnp.abs(x)))


def _ada_kernel(c_ref, w_ref, b_ref, o_ref):
    sc = _silu(c_ref[...])
    o_ref[...] = jnp.dot(sc.astype(BF16), w_ref[...].astype(BF16),
                         preferred_element_type=F32) + b_ref[...]


def _ada(cc, w, b):
    rows, d = cc.shape
    n = w.shape[1]
    tn = 1536
    return pl.pallas_call(
        _ada_kernel,
        grid=(n // tn,),
        in_specs=[pl.BlockSpec((rows, d), lambda j: (0, 0)),
                  pl.BlockSpec((d, tn), lambda j: (0, j)),
                  pl.BlockSpec((1, tn), lambda j: (0, j))],
        out_specs=pl.BlockSpec((rows, tn), lambda j: (0, j)),
        out_shape=jax.ShapeDtypeStruct((rows, n), F32),
        compiler_params=_cparams(("arbitrary",), 32),
        name="ada",
    )(cc, w, b)


def _inproj_kernel(x_ref, mod_ref, nw_ref, w_ref, main_ref, small_ref, h_ref):
    j = pl.program_id(2)

    @pl.when(j == 0)
    def _():
        x = x_ref[...]
        r = lax.rsqrt(jnp.mean(x * x, axis=-1, keepdims=True) + EPS)
        shift = mod_ref[:, 0:D_MODEL]
        scale = mod_ref[:, D_MODEL:2 * D_MODEL]
        h_ref[...] = ((x * r * nw_ref[...]) * (1.0 + scale) + shift).astype(BF16)

    acc = jnp.dot(h_ref[...], w_ref[...], preferred_element_type=F32)
    main_ref[...] = acc.astype(BF16)

    @pl.when(j == MAIN_W // IN_MM_TN - 1)
    def _():
        lo = COL_SMALL * IN_TN - (MAIN_W - IN_MM_TN)
        small_ref[...] = acc[:, lo:lo + SMALL_W]


def _inproj(x, mod, nw, w_cat, tm):
    bm, t, d = x.shape
    return pl.pallas_call(
        _inproj_kernel,
        grid=(bm, t // tm, MAIN_W // IN_MM_TN),
        in_specs=[pl.BlockSpec((None, tm, d), lambda b, i, j: (b, i, 0)),
                  pl.BlockSpec((None, 1, 6 * d), lambda b, i, j: (b, 0, 0)),
                  pl.BlockSpec((1, d), lambda b, i, j: (0, 0)),
                  pl.BlockSpec((d, IN_MM_TN), lambda b, i, j: (0, j))],
        out_specs=[pl.BlockSpec((None, tm, IN_MM_TN), lambda b, i, j: (b, i, j)),
                   pl.BlockSpec((None, tm, SMALL_W), lambda b, i, j: (b, i, 0))],
        out_shape=[jax.ShapeDtypeStruct((bm, t, MAIN_W), BF16),
                   jax.ShapeDtypeStruct((bm, t, SMALL_W), F32)],
        scratch_shapes=[pltpu.VMEM((tm, d), BF16)],
        compiler_params=_cparams(("arbitrary", "arbitrary", "arbitrary"), 40),
        name="inproj",
    )(x, mod, nw, w_cat)


CONV_HALO = 16


CONV_SUB = 128
CONV_TC = 2048


def _conv_shift_matrices():
    s = np.zeros((CONV_W, CONV_SUB, CONV_SUB + CONV_HALO), np.float32)
    for k in range(CONV_W):
        for t in range(CONV_SUB):
            src = t + k - CONV_W // 2
            col = src if 0 <= src < CONV_SUB else CONV_SUB + 8 + (src if src < 0 else src - CONV_SUB)
            s[k, t, col] = 1.0
    return jnp.asarray(s, BF16)


def _conv_kernel(cur_ref, prev_ref, next_ref, s_ref, w_ref, b_ref, o_ref, *, nt, tt):
    i = pl.program_id(1)
    pv = jnp.where(i > 0, prev_ref[...].astype(F32)[8:16], 0.0)
    nx = jnp.where(i < nt - 1, next_ref[...].astype(F32)[0:8], 0.0)
    mid = CONV_W // 2
    for hb in range(tt // CONV_SUB):
        r0 = hb * CONV_SUB
        cur = cur_ref[r0:r0 + CONV_SUB, :]
        before = pv if hb == 0 else cur_ref[r0 - CONV_HALO:r0, :].astype(F32)[8:16]
        after = nx if r0 + CONV_SUB == tt else cur_ref[r0 + CONV_SUB:r0 + CONV_SUB + CONV_HALO, :].astype(F32)[0:8]
        ext = jnp.concatenate([cur, jnp.concatenate([before, after], axis=0).astype(BF16)], axis=0)
        acc = b_ref[...] + w_ref[mid:mid + 1, :] * cur.astype(F32)
        for k in range(CONV_W):
            if k != mid:
                acc = acc + w_ref[k:k + 1, :] * jnp.dot(s_ref[k], ext, preferred_element_type=F32)
        o_ref[r0:r0 + CONV_SUB, :] = _silu(acc).astype(BF16)


def _conv(main, conv_w, conv_b, tt):
    b, t, _ = main.shape
    tc = CONV_TC
    nt = t // tt
    hb = tt // CONV_HALO
    last = t // CONV_HALO - 1
    col0 = COL_XBC * IN_TN // tc
    shifts = _conv_shift_matrices()
    return pl.pallas_call(
        functools.partial(_conv_kernel, nt=nt, tt=tt),
        grid=(b, nt, XBC_DIM // tc),
        in_specs=[pl.BlockSpec((None, tt, tc), lambda bb, i, j: (bb, i, col0 + j)),
                  pl.BlockSpec((None, CONV_HALO, tc),
                               lambda bb, i, j: (bb, jnp.maximum(i * hb - 1, 0), col0 + j)),
                  pl.BlockSpec((None, CONV_HALO, tc),
                               lambda bb, i, j: (bb, jnp.minimum((i + 1) * hb, last), col0 + j)),
                  pl.BlockSpec(shifts.shape, lambda bb, i, j: (0, 0, 0)),
                  pl.BlockSpec((8, tc), lambda bb, i, j: (0, j)),
                  pl.BlockSpec((1, tc), lambda bb, i, j: (0, j))],
        out_specs=pl.BlockSpec((None, tt, tc), lambda bb, i, j: (bb, i, j)),
        out_shape=jax.ShapeDtypeStruct((b, t, XBC_DIM), BF16),
        compiler_params=_cparams(("arbitrary", "arbitrary", "arbitrary"), 32),
        name="conv",
    )(main, main, main, shifts, conv_w, conv_b)


N_PAIRS = SSD_HEADS // 2


def _ssd_kernel(*refs, nc, has_init):
    if has_init:
        xs_ref, b_ref, c_ref, dt_ref, prm_ref, init_ref, y_ref, fin_ref, st_ref = refs
    else:
        xs_ref, b_ref, c_ref, dt_ref, prm_ref, y_ref, fin_ref, st_ref = refs
    d = pl.program_id(0)
    c = pl.program_id(2)

    @pl.when(c == 0)
    def _():
        for pp in range(N_PAIRS):
            if has_init:
                st_ref[pp] = init_ref[pp * LANES:(pp + 1) * LANES, :].T
            else:
                st_ref[pp] = jnp.zeros((D_STATE, LANES), F32)

    row = lax.broadcasted_iota(jnp.int32, (CHUNK, CHUNK), 0)
    col = lax.broadcasted_iota(jnp.int32, (CHUNK, CHUNK), 1)
    tri = (row - col) * (1 - 2 * d) >= 0
    lo = col < SSD_HEADDIM

    a_log = prm_ref[0:1, :]
    bias = prm_ref[1:2, :]
    dt = _softplus(dt_ref[...] + bias)
    a = dt * (-jnp.exp(a_log))
    cs = jnp.dot(tri.astype(F32), a, precision=lax.Precision.HIGHEST,
                 preferred_element_type=F32)
    tot = jnp.sum(a, axis=0, keepdims=True)
    w_t = (jnp.exp(tot - cs) * dt).T
    r_t = (cs - jnp.log(dt)).T
    dec = jnp.exp(tot)

    for g in range(SSD_GROUPS):
        bg = b_ref[:, g * D_STATE:(g + 1) * D_STATE]
        cg = c_ref[:, g * D_STATE:(g + 1) * D_STATE]
        sc = lax.dot_general(cg, bg, NT_DIMS, preferred_element_type=F32)
        bg_t = bg.astype(F32).T
        cgf = cg.astype(F32)
        for q in range(2):
            pp = g * 2 + q
            xp = xs_ref[:, pp * LANES:(pp + 1) * LANES]
            st_old = st_ref[pp]
            rhs = jnp.concatenate([xp, st_old.astype(BF16)], axis=0)
            ys, us = [], []
            for e in range(2):
                h = pp * 2 + e
                csb = jnp.broadcast_to(cs[:, h:h + 1], (CHUNK, CHUNK))
                m = sc * jnp.exp(jnp.where(tri, csb - r_t[h:h + 1, :], -jnp.inf))
                cse = cgf * jnp.exp(csb)
                lhs = jnp.concatenate([m.astype(BF16), cse.astype(BF16)], axis=1)
                ys.append(jnp.dot(lhs, rhs, preferred_element_type=F32))
                bw = (bg_t * w_t[h:h + 1, :]).astype(BF16)
                us.append(jnp.dot(bw, xp, preferred_element_type=F32))
            y_ref[:, pp * LANES:(pp + 1) * LANES] = jnp.where(lo, ys[0], ys[1]).astype(BF16)
            decrow = jnp.where(lo[0:1, :], dec[:, 2 * pp:2 * pp + 1], dec[:, 2 * pp + 1:2 * pp + 2])
            st_ref[pp] = st_old * decrow + jnp.where(lo, us[0], us[1])

    @pl.when(c == nc - 1)
    def _():
        for pp in range(N_PAIRS):
            fin_ref[pp * LANES:(pp + 1) * LANES, :] = st_ref[pp].T


def _ssd(xbc, small, prm, init):
    b, t, _ = xbc.shape
    nc = t // CHUNK
    has_init = init is not None

    def ce(d, c):
        return c + d * (nc - 1 - 2 * c)

    in_specs = [pl.BlockSpec((None, CHUNK, SSD_D_INNER), lambda d, bb, c: (bb, ce(d, c), 0)),
                pl.BlockSpec((None, CHUNK, SSD_GROUPS * D_STATE), lambda d, bb, c: (bb, ce(d, c), 2)),
                pl.BlockSpec((None, CHUNK, SSD_GROUPS * D_STATE), lambda d, bb, c: (bb, ce(d, c), 3)),
                pl.BlockSpec((None, CHUNK, LANES), lambda d, bb, c: (bb, ce(d, c), 4 + d)),
                pl.BlockSpec((None, 8, LANES), lambda d, bb, c: (d, 0, 0))]
    args = [xbc, xbc, xbc, small, prm]
    if has_init:
        in_specs.append(pl.BlockSpec((None, None, SSD_D_INNER, D_STATE), lambda d, bb, c: (d, bb, 0, 0)))
        args.append(init)
    return pl.pallas_call(
        functools.partial(_ssd_kernel, nc=nc, has_init=has_init),
        grid=(2, b, nc),
        in_specs=in_specs,
        out_specs=[pl.BlockSpec((None, None, CHUNK, SSD_D_INNER), lambda d, bb, c: (d, bb, ce(d, c), 0)),
                   pl.BlockSpec((None, None, SSD_D_INNER, D_STATE), lambda d, bb, c: (d, bb, 0, 0))],
        out_shape=[jax.ShapeDtypeStruct((2, b, t, SSD_D_INNER), BF16),
                   jax.ShapeDtypeStruct((2, b, SSD_D_INNER, D_STATE), F32)],
        scratch_shapes=[pltpu.VMEM((N_PAIRS, D_STATE, LANES), F32)],
        compiler_params=_cparams(("arbitrary", "arbitrary", "arbitrary"), 40),
        name="ssd",
    )(*args)


def _rope(x, cos, sin_a, sin_b, reps):
    n = x.shape[1]
    half = HEAD_DIM // 4
    cosf = jnp.concatenate([cos] * reps, axis=1)
    saf = jnp.concatenate([sin_a] * reps, axis=1)
    sbf = jnp.concatenate([sin_b] * reps, axis=1)
    return x * cosf + pltpu.roll(x, n - half, 1) * saf + pltpu.roll(x, half, 1) * sbf


def _prep_kernel(*refs, rope):
    if rope:
        q_ref, kv_ref, qnw_ref, knw_ref, bq_ref, bk_ref, cos_ref, sa_ref, sb_ref, qo_ref, ko_ref, vo_ref = refs
    else:
        q_ref, kv_ref, qnw_ref, knw_ref, bq_ref, bk_ref, qo_ref, ko_ref, vo_ref = refs
    q = q_ref[...].astype(F32)
    k = kv_ref[:, 0:KV_DIM]
    vo_ref[...] = kv_ref[:, KV_DIM:2 * KV_DIM]
    qss = jnp.dot((q * q).astype(BF16), bq_ref[...], preferred_element_type=F32)
    kss = jnp.dot((k * k).astype(BF16), bk_ref[...], preferred_element_type=F32)
    qn = q * lax.rsqrt(qss * (1.0 / HEAD_DIM) + EPS) * qnw_ref[...]
    kn = k * lax.rsqrt(kss * (1.0 / HEAD_DIM) + EPS) * knw_ref[...]
    if rope:
        cos, sa, sb = cos_ref[...], sa_ref[...], sb_ref[...]
        qn = _rope(qn, cos, sa, sb, D_MODEL // LANES)
        kn = _rope(kn, cos, sa, sb, KV_DIM // LANES)
    qo_ref[...] = (qn * (HEAD_DIM ** -0.5)).T.astype(BF16)
    ko_ref[...] = kn


def _prep(main, small, qnw_t, knw_t, bq, bk, tables, tt):
    b, t, _ = main.shape
    rope = tables is not None
    in_specs = [pl.BlockSpec((None, tt, D_MODEL), lambda bb, i: (bb, i, COL_Q)),
                pl.BlockSpec((None, tt, 2 * KV_DIM), lambda bb, i: (bb, i, 0)),
                pl.BlockSpec((1, D_MODEL), lambda bb, i: (0, 0)),
                pl.BlockSpec((1, KV_DIM), lambda bb, i: (0, 0)),
                pl.BlockSpec((D_MODEL, D_MODEL), lambda bb, i: (0, 0)),
                pl.BlockSpec((KV_DIM, KV_DIM), lambda bb, i: (0, 0))]
    args = [main, small, qnw_t, knw_t, bq, bk]
    if rope:
        in_specs += [pl.BlockSpec((tt, LANES), lambda bb, i: (i, 0))] * 3
        args += list(tables)
    return pl.pallas_call(
        functools.partial(_prep_kernel, rope=rope),
        grid=(b, t // tt),
        in_specs=in_specs,
        out_specs=[pl.BlockSpec((None, D_MODEL, tt), lambda bb, i: (bb, 0, i)),
                   pl.BlockSpec((None, tt, KV_DIM), lambda bb, i: (bb, i, 0)),
                   pl.BlockSpec((None, tt, KV_DIM), lambda bb, i: (bb, i, 0))],
        out_shape=[jax.ShapeDtypeStruct((b, D_MODEL, t), BF16),
                   jax.ShapeDtypeStruct((b, t, KV_DIM), F32),
                   jax.ShapeDtypeStruct((b, t, KV_DIM), F32)],
        compiler_params=_cparams(("arbitrary", "arbitrary"), 32),
        name="qk_prep",
    )(*args)


ATT_KC = 256
ATT_VROWS = 80


def _attn_kernel(*refs, has_ctx):
    if has_ctx:
        qt_ref, k_ref, v_ref, ck_ref, cv_ref, o_ref, ks_ref, vt_ref, ot_ref, sa_ref, sb_ref = refs
    else:
        qt_ref, k_ref, v_ref, o_ref, ks_ref, vt_ref, ot_ref, sa_ref, sb_ref = refs
    qi = pl.program_id(1)
    tq = qt_ref.shape[1]

    @pl.when(qi == 0)
    def _():
        parts = [(k_ref, v_ref, 0)]
        if has_ctx:
            parts.append((ck_ref, cv_ref, k_ref.shape[0]))
        for kr, vr, off in parts:
            n = kr.shape[0]
            lo = lax.broadcasted_iota(jnp.int32, (n, LANES), 1) < HEAD_DIM
            ones_pad = (lax.broadcasted_iota(jnp.int32, (ATT_VROWS - HEAD_DIM, n), 0) == 0).astype(F32)
            for pr in range(KV_DIM // LANES):
                a = kr[:, pr * LANES:(pr + 1) * LANES]
                ar = pltpu.roll(a, HEAD_DIM, 1)
                ks_ref[2 * pr, off:off + n, :] = jnp.where(lo, a, 0.0).astype(BF16)
                ks_ref[2 * pr + 1, off:off + n, :] = jnp.where(lo, ar, 0.0).astype(BF16)
                vt = vr[:, pr * LANES:(pr + 1) * LANES].T
                for e in range(2):
                    vt_ref[2 * pr + e, :, off:off + n] = jnp.concatenate(
                        [vt[e * HEAD_DIM:(e + 1) * HEAD_DIM], ones_pad], axis=0).astype(BF16)

    zq = jnp.zeros((LANES - HEAD_DIM, tq), BF16)
    hpg = N_HEADS // N_KV_HEADS
    tk = ks_ref.shape[1]
    chunks = [slice(c, min(c + ATT_KC, tk)) for c in range(0, tk, ATT_KC)]

    def scores(dst_ref, h):
        row = pl.multiple_of(h * HEAD_DIM, HEAD_DIM)
        qx = jnp.concatenate([qt_ref[pl.ds(row, HEAD_DIM), :], zq], axis=0)
        for c in chunks:
            dst_ref[c, :] = jnp.dot(ks_ref[h // hpg, c, :], qx, preferred_element_type=F32)

    def softmax_pv(s_ref, h):
        m = None
        for c in chunks:
            pm = jnp.max(s_ref[c, :], axis=0, keepdims=True)
            m = pm if m is None else jnp.maximum(m, pm)
        p = jnp.exp(s_ref[...] - m).astype(BF16)
        ox = jnp.dot(vt_ref[h // hpg], p, preferred_element_type=F32)
        row = pl.multiple_of(h * HEAD_DIM, HEAD_DIM)
        ot_ref[pl.ds(row, HEAD_DIM), :] = ox[0:HEAD_DIM] * (1.0 / ox[HEAD_DIM:HEAD_DIM + 1])

    scores(sa_ref, 0)

    def pair_body(hp, carry):
        h = 2 * hp
        scores(sb_ref, h + 1)
        softmax_pv(sa_ref, h)
        scores(sa_ref, jnp.minimum(h + 2, N_HEADS - 1))
        softmax_pv(sb_ref, h + 1)
        return carry

    lax.fori_loop(0, N_HEADS // 2, pair_body, 0)
    o_ref[...] = ot_ref[...].T.astype(BF16)


def _attn(qt, k, v, ctx, tq):
    b, _, t = qt.shape
    has_ctx = ctx is not None
    tk = k.shape[1] + (ctx[0].shape[1] if has_ctx else 0)
    kv_spec = lambda a: pl.BlockSpec((None, a.shape[1], KV_DIM), lambda bb, i: (bb, 0, 0))
    in_specs = [pl.BlockSpec((None, D_MODEL, tq), lambda bb, i: (bb, 0, i)), kv_spec(k), kv_spec(v)]
    args = [qt, k, v]
    if has_ctx:
        in_specs += [kv_spec(ctx[0]), kv_spec(ctx[1])]
        args += list(ctx)
    return pl.pallas_call(
        functools.partial(_attn_kernel, has_ctx=has_ctx),
        grid=(b, t // tq),
        in_specs=in_specs,
        out_specs=pl.BlockSpec((None, tq, D_MODEL), lambda bb, i: (bb, i, 0)),
        out_shape=jax.ShapeDtypeStruct((b, t, D_MODEL), BF16),
        scratch_shapes=[pltpu.VMEM((N_KV_HEADS, tk, LANES), BF16),
                        pltpu.VMEM((N_KV_HEADS, ATT_VROWS, tk), BF16),
                        pltpu.VMEM((D_MODEL, tq), F32),
                        pltpu.VMEM((tk, tq), F32), pltpu.VMEM((tk, tq), F32)],
        compiler_params=_cparams(("arbitrary", "arbitrary"), 48),
        name="attn",
    )(*args)


def _merge_kernel(yf_ref, yb_ref, xs_ref, z_ref, o_ref, ga_ref, gb_ref, x_ref, mod_ref,
                  dsk_ref, snw_ref, wso_ref, wao_ref, wo_ref, x1_ref):
    y = yf_ref[...].astype(F32) + yb_ref[...].astype(F32) + dsk_ref[...] * xs_ref[...].astype(F32)
    y = y * _silu(z_ref[...].astype(F32))
    y = y * lax.rsqrt(jnp.mean(y * y, axis=-1, keepdims=True) + EPS) * snw_ref[...]
    ya = jnp.dot(y.astype(BF16), wso_ref[...], preferred_element_type=F32)
    yb2 = jnp.dot(o_ref[...], wao_ref[...], preferred_element_type=F32)
    merged = _sigmoid(ga_ref[...].astype(F32)) * ya + _sigmoid(gb_ref[...].astype(F32)) * yb2
    mix = jnp.dot(merged.astype(BF16), wo_ref[...], preferred_element_type=F32)
    gate1 = mod_ref[:, 2 * D_MODEL:3 * D_MODEL]
    x1_ref[...] = x_ref[...] + gate1 * mix


def _merge(y2, xbc, main, o, x, mod, dsk, snw, wso, wao, wo, tm):
    bm, t, d = x.shape
    tok = lambda w, cb: pl.BlockSpec((None, tm, w), lambda b, i: (b, i, cb))
    full = lambda a: pl.BlockSpec(a.shape, lambda b, i: (0,) * a.ndim)
    return pl.pallas_call(
        _merge_kernel,
        grid=(bm, t // tm),
        in_specs=[pl.BlockSpec((None, None, tm, SSD_D_INNER), lambda b, i: (0, b, i, 0)),
                  pl.BlockSpec((None, None, tm, SSD_D_INNER), lambda b, i: (1, b, i, 0)),
                  tok(SSD_D_INNER, 0), tok(SSD_D_INNER, COL_Z), tok(d, 0),
                  tok(d, COL_GA), tok(d, COL_GB), tok(d, 0),
                  pl.BlockSpec((None, 1, 6 * d), lambda b, i: (b, 0, 0)),
                  full(dsk), full(snw), full(wso), full(wao), full(wo)],
        out_specs=tok(d, 0),
        out_shape=jax.ShapeDtypeStruct((bm, t, d), F32),
        compiler_params=_cparams(("arbitrary", "arbitrary"), 48),
        name="merge",
    )(y2, y2, xbc, main, o, main, main, x, mod, dsk, snw, wso, wao, wo)


def _top_rows(s, k, ranks=False):
    rows = []
    rank = jnp.full(s.shape, float(k), F32) if ranks else None
    for it in range(k):
        m = jnp.max(s, axis=0, keepdims=True)
        rows.append(m)
        hit = s == m
        s = jnp.where(hit, -jnp.inf, s)
        if ranks:
            rank = jnp.where(hit, float(it), rank)
    return (rows, rank) if ranks else rows


def _peer_front_kernel(x_ref, mod_ref, nw_ref, wpq_ref, keys_ref, ht_ref, a1_ref, n1_ref, r2_ref, e2_ref):
    x = x_ref[...]
    r = lax.rsqrt(jnp.mean(x * x, axis=-1, keepdims=True) + EPS)
    shift = mod_ref[:, 3 * D_MODEL:4 * D_MODEL]
    scale = mod_ref[:, 4 * D_MODEL:5 * D_MODEL]
    h = (x * r * nw_ref[...]) * (1.0 + scale) + shift
    hb = h.astype(BF16)
    ht_ref[...] = h.T.astype(BF16)
    qp = jnp.dot(hb, wpq_ref[...], preferred_element_type=F32).astype(BF16)
    k = PEER_TOPK
    for hd in range(PEER_HEADS):
        s1 = lax.dot_general(keys_ref[2 * hd], qp[:, (2 * hd) * LANES:(2 * hd + 1) * LANES],
                             NT_DIMS, preferred_element_type=F32)
        s2 = lax.dot_general(keys_ref[2 * hd + 1], qp[:, (2 * hd + 1) * LANES:(2 * hd + 2) * LANES],
                             NT_DIMS, preferred_element_type=F32)
        ta = _top_rows(s1, k)
        tb_rows, rank2 = _top_rows(s2, k, ranks=True)
        tb = jnp.concatenate(tb_rows, axis=0)
        pieces = [ta[0] + tb] + [ta[p] + tb[0:8] for p in range(1, 8)]
        pieces.append(jnp.concatenate(ta[8:], axis=0) + tb[0:1])
        best = _top_rows(jnp.concatenate(pieces, axis=0), k)
        zsum = jnp.ones_like(best[0])
        for v in best[1:]:
            zsum = zsum + jnp.exp(v - best[0])
        thr = best[-1]
        ta_arr = jnp.concatenate(ta, axis=0)
        n1 = jnp.zeros_like(s1)
        for q in range(k):
            theta = jnp.min(jnp.where(ta_arr + tb_rows[q] >= thr, ta_arr, jnp.inf), axis=0, keepdims=True)
            n1 = jnp.where(s1 >= theta, float(q + 1), n1)
        a1_ref[hd] = jnp.exp(s1 - ta[0])
        n1_ref[hd] = n1
        r2_ref[hd] = rank2.astype(BF16)
        e2_ref[hd] = (jnp.exp(s2 - tb_rows[0]) / zsum).astype(BF16)


def _peer_front(x1, mod, nw, wpq, keys, tm):
    bm, t, d = x1.shape
    sel = lambda: pl.BlockSpec((None, PEER_HEADS, N_KEYS, tm), lambda b, i: (b, 0, 0, i))
    sel_shape = lambda dt: jax.ShapeDtypeStruct((bm, PEER_HEADS, N_KEYS, t), dt)
    return pl.pallas_call(
        _peer_front_kernel,
        grid=(bm, t // tm),
        in_specs=[pl.BlockSpec((None, tm, d), lambda b, i: (b, i, 0)),
                  pl.BlockSpec((None, 1, 6 * d), lambda b, i: (b, 0, 0)),
                  pl.BlockSpec((1, d), lambda b, i: (0, 0)),
                  pl.BlockSpec(wpq.shape, lambda b, i: (0, 0)),
                  pl.BlockSpec(keys.shape, lambda b, i: (0, 0, 0))],
        out_specs=[pl.BlockSpec((None, d, tm), lambda b, i: (b, 0, i)), sel(), sel(), sel(), sel()],
        out_shape=[jax.ShapeDtypeStruct((bm, d, t), BF16), sel_shape(F32), sel_shape(F32),
                   sel_shape(BF16), sel_shape(BF16)],
        compiler_params=_cparams(("arbitrary", "arbitrary"), 48),
        name="peer_front",
    )(x1, mod, nw, wpq, keys)


PEER_EB = 2048
PEER_RB = 16
PEER_LW = 256
PEER_VK = 4


def _gelu_tanh(x):
    return 0.5 * x * (1.0 + jnp.tanh(0.7978845608028654 * (x + 0.044715 * (x * x * x))))


def _peer_dense_kernel(ht_ref, a1_ref, n1_ref, r2_ref, e2_ref, u_ref, v_ref, x1_ref, mod_ref, fw_ref,
                       y_ref, acc_ref, gt_ref, *, ne):
    e = pl.program_id(2)
    tt = ht_ref.shape[1]
    n_sub = PEER_EB // N_KEYS

    @pl.when(e == 0)
    def _():
        acc_ref[...] = jnp.zeros_like(acc_ref)

    n_rb = N_KEYS // PEER_RB

    def activations(sub):
        act = jnp.dot(u_ref[sub * N_KEYS:(sub + 1) * N_KEYS, :], ht_ref[...], preferred_element_type=F32)
        return _gelu_tanh(act.astype(BF16))

    def v_matmul(sub_hi):
        ch = slice((sub_hi + 1 - PEER_VK) * N_KEYS, (sub_hi + 1) * N_KEYS)
        acc_ref[...] += lax.dot_general(v_ref[ch, :], gt_ref[ch, :], (((0,), (0,)), ((), ())),
                                        preferred_element_type=F32)

    g_next = activations(0)
    for sub in range(n_sub):
        g = g_next
        if sub + 1 < n_sub:
            g_next = activations(sub + 1)
        if sub >= PEER_VK and sub % PEER_VK == 0:
            v_matmul(sub - 1)
        i = e * n_sub + sub
        a1_rows = [a1_ref[hd, pl.ds(i, 1), :] for hd in range(PEER_HEADS)]
        n1_rows = [n1_ref[hd, pl.ds(i, 1), :] for hd in range(PEER_HEADS)]
        for lg in range(tt // PEER_LW):
            ls = slice(lg * PEER_LW, (lg + 1) * PEER_LW)
            w = [None] * n_rb
            for hd in range(PEER_HEADS):
                a1 = jnp.broadcast_to(a1_rows[hd][:, ls], (PEER_RB, PEER_LW)).astype(BF16)
                n1 = jnp.broadcast_to(n1_rows[hd][:, ls], (PEER_RB, PEER_LW)).astype(BF16)
                for rb in range(n_rb):
                    rows = slice(rb * PEER_RB, (rb + 1) * PEER_RB)
                    wh = a1 * jnp.where(r2_ref[hd, rows, ls] < n1, e2_ref[hd, rows, ls], jnp.zeros((), BF16))
                    w[rb] = wh if w[rb] is None else w[rb] + wh
            for rb in range(n_rb):
                rows = slice(rb * PEER_RB, (rb + 1) * PEER_RB)
                gt_ref[sub * N_KEYS + rb * PEER_RB:sub * N_KEYS + (rb + 1) * PEER_RB, ls] = w[rb] * g[rows, ls]
    v_matmul(n_sub - 1)

    @pl.when(e == ne - 1)
    def _():
        gate2 = mod_ref[:, 5 * D_MODEL:6 * D_MODEL]
        x2 = x1_ref[...] + gate2 * acc_ref[...].T
        y_ref[...] = x2 * lax.rsqrt(jnp.mean(x2 * x2, axis=-1, keepdims=True) + EPS) * fw_ref[...]


def _peer_dense(ht, a1, n1, r2, e2, u, v, x1, mod, fw, tt):
    bm, t, d = x1.shape
    ne = u.shape[0] // PEER_EB
    sel = lambda: pl.BlockSpec((None, PEER_HEADS, N_KEYS, tt), lambda b, i, e: (b, 0, 0, i))
    return pl.pallas_call(
        functools.partial(_peer_dense_kernel, ne=ne),
        grid=(bm, t // tt, ne),
        in_specs=[pl.BlockSpec((None, d, tt), lambda b, i, e: (b, 0, i)),
                  sel(), sel(), sel(), sel(),
                  pl.BlockSpec((PEER_EB, d), lambda b, i, e: (e, 0)),
                  pl.BlockSpec((PEER_EB, d), lambda b, i, e: (e, 0)),
                  pl.BlockSpec((None, tt, d), lambda b, i, e: (b, i, 0)),
                  pl.BlockSpec((None, 1, 6 * d), lambda b, i, e: (b, 0, 0)),
                  pl.BlockSpec((1, d), lambda b, i, e: (0, 0))],
        out_specs=pl.BlockSpec((None, tt, d), lambda b, i, e: (b, i, 0)),
        out_shape=jax.ShapeDtypeStruct((bm, t, d), F32),
        scratch_shapes=[pltpu.VMEM((d, tt), F32), pltpu.VMEM((PEER_EB, tt), BF16)],
        compiler_params=_cparams(("arbitrary", "arbitrary", "arbitrary"), 56),
        name="peer_dense",
    )(ht, a1, n1, r2, e2, u, v, x1, mod, fw)


def _rope_tables(t):
    pos_t = np.arange(t)
    j = np.arange(HEAD_DIM)
    blk = j // (HEAD_DIM // 2)
    jj = j % (HEAD_DIM // 2)
    half = HEAD_DIM // 4
    inv = ROPE_THETA ** (-(jj % half).astype(np.float32) / half)
    pos = jnp.where(blk[None, :] == 0, (pos_t // GRID_W)[:, None], (pos_t % GRID_W)[:, None]).astype(F32)
    ang = pos * jnp.asarray(inv, F32)[None, :]
    cos, sin = jnp.cos(ang), jnp.sin(ang)
    first = jnp.asarray(jj < half)[None, :]
    sin_a = jnp.where(first, -sin, 0.0)
    sin_b = jnp.where(first, 0.0, sin)
    rep = lambda a: jnp.concatenate([a, a], axis=1)
    return rep(cos), rep(sin_a), rep(sin_b)


def _block_ones(n):
    idx = np.arange(n) // HEAD_DIM
    return jnp.asarray(idx[:, None] == idx[None, :], BF16)


def _group(x, mod, init, ctx_kv, seq_shape, wts, tables):
    bm, tmod, d = x.shape
    b, t = seq_shape
    seq = lambda a: a.reshape((b, t) + a.shape[2:])
    tok = lambda a: a.reshape((bm, tmod) + a.shape[2:])

    main, small = _inproj(x, mod, wts["norm1_w"], wts["w_cat"], 512)
    xbc = _conv(seq(main), wts["conv_w"], wts["conv_b"], 256)
    y2, fin = _ssd(xbc, seq(small), wts["ssd_prm"], init)
    qn, kn, vn = _prep(seq(main), seq(small), wts["qnw_t"], wts["knw_t"], wts["bq"], wts["bk"], tables, 256)
    o = _attn(qn, kn, vn, ctx_kv, 256)
    y2m = y2.reshape((2, bm, tmod, SSD_D_INNER))
    x1 = _merge(y2m, tok(xbc), main, tok(o), x, mod, wts["dsk"], wts["ssd_norm_w"],
                wts["w_ssd_out"], wts["w_attn_out"], wts["w_o"], 256)
    ht, a1, n1, r2, e2 = _peer_front(x1, mod, wts["norm2_w"], wts["w_pq"], wts["peer_keys"], 256)
    y = _peer_dense(ht, a1, n1, r2, e2, wts["peer_u"], wts["peer_v"], x1, mod, wts["final_norm_w"], 512)
    return y, fin, kn, vn


def kernel(x_prompt, x_sample, cache_k, cache_v, state_ssm_fwd, state_ssm_bwd, c, c_ctx, w_ada, b_ada, norm1_w, norm2_w, w_in, conv_w, conv_b, a_log_f, a_log_b, dt_bias_f, dt_bias_b, d_skip, ssd_norm_w, w_ssd_out, q_norm_w, k_norm_w, w_attn_out, w_o, w_pq, peer_keys, peer_u, peer_v, final_norm_w):
    assert w_in.shape[0] == 1, "single-layer problem"
    bp, tp, d = x_prompt.shape
    bs, ts, _ = x_sample.shape

    wi = w_in[0]
    offs = np.cumsum([0, SSD_D_INNER, XBC_DIM, SSD_HEADS, SSD_HEADS, D_MODEL, KV_DIM, KV_DIM, D_MODEL, D_MODEL])
    part = lambda n: wi[:, offs[n]:offs[n + 1]]
    zpad = lambda n: jnp.zeros((d, n), F32)
    w_cat = jnp.concatenate(
        [part(0), part(1), part(4), part(7), part(8),
         part(5), part(6),
         part(2), zpad(LANES - SSD_HEADS), part(3), zpad(LANES - SSD_HEADS),
         zpad(SMALL_W - 2 * KV_DIM - 2 * LANES)], axis=1).astype(BF16)
    row = lambda a: a.reshape(1, -1).astype(F32)
    padl = lambda a: jnp.pad(a.astype(F32), (0, LANES - a.shape[0]))
    prm = jnp.zeros((2, 8, LANES), F32)
    prm = prm.at[0, 0].set(padl(a_log_f[0])).at[0, 1].set(padl(dt_bias_f[0]))
    prm = prm.at[1, 0].set(padl(a_log_b[0])).at[1, 1].set(padl(dt_bias_b[0]))
    wts = {
        "norm1_w": row(norm1_w[0]), "norm2_w": row(norm2_w[0]), "w_cat": w_cat,
        "conv_w": jnp.pad(conv_w[0], ((0, 8 - CONV_W), (0, 0))), "conv_b": row(conv_b[0]),
        "ssd_prm": prm,
        "dsk": row(jnp.repeat(d_skip[0], SSD_HEADDIM)), "ssd_norm_w": row(ssd_norm_w[0]),
        "w_ssd_out": w_ssd_out[0].astype(BF16), "w_attn_out": w_attn_out[0].astype(BF16),
        "w_o": w_o[0].astype(BF16),
        "qnw_t": row(jnp.tile(q_norm_w[0], N_HEADS)), "knw_t": row(jnp.tile(k_norm_w[0], N_KV_HEADS)),
        "bq": _block_ones(D_MODEL), "bk": _block_ones(KV_DIM),
        "w_pq": w_pq[0].astype(BF16),
        "peer_keys": peer_keys[0].reshape(PEER_HEADS * 2, N_KEYS, PEER_DQ // 2).astype(BF16),
        "peer_u": peer_u[0].astype(BF16), "peer_v": peer_v[0].astype(BF16),
        "final_norm_w": row(final_norm_w),
    }

    cc = jnp.concatenate([c_ctx[None, :], c, jnp.zeros((8 - 1 - bs, d), F32)], axis=0)
    mod = _ada(cc, w_ada[0], row(b_ada[0]))
    mod_p = mod[0:1].reshape(1, 1, 6 * d)
    mod_s = mod[1:1 + bs].reshape(bs, 1, 6 * d)

    yp, fin_p, kn_p, vn_p = _group(x_prompt.reshape(1, bp * tp, d), mod_p, None, None, (bp, tp), wts, None)

    init = jnp.stack([state_ssm_fwd[:, 0], state_ssm_bwd[:, 0]], axis=0).reshape(2, bs, SSD_D_INNER, D_STATE)
    ctx_kv = (cache_k[:, 0].reshape(bs, -1, KV_DIM), cache_v[:, 0].reshape(bs, -1, KV_DIM))
    ys, _, _, _ = _group(x_sample, mod_s, init, ctx_kv, (bs, ts), wts, _rope_tables(ts))

    y_prompt = yp.reshape(bp, tp, d)
    new_cache_k = kn_p.reshape(bp, 1, tp, N_KV_HEADS, HEAD_DIM)
    new_cache_v = vn_p.reshape(bp, 1, tp, N_KV_HEADS, HEAD_DIM)
    new_state_fwd = fin_p[0].reshape(bp, 1, SSD_HEADS, SSD_HEADDIM, D_STATE)
    new_state_bwd = fin_p[1].reshape(bp, 1, SSD_HEADS, SSD_HEADDIM, D_STATE)
    return (y_prompt, ys, new_cache_k, new_cache_v, new_state_fwd, new_state_bwd)
```

```python
import functools

import jax
import jax.numpy as jnp
import numpy as np
from jax import lax
from jax.experimental import pallas as pl
from jax.experimental.pallas import tpu as pltpu

F32 = jnp.float32
BF16 = jnp.bfloat16

D_MODEL = 1024
GRID_W = 64
SSD_D_INNER = 2048
SSD_HEADDIM = 64
SSD_HEADS = 32
SSD_GROUPS = 8
D_STATE = 128
CONV_W = 5
CHUNK = 128
XBC_DIM = 4096
N_HEADS = 16
N_KV_HEADS = 4
HEAD_DIM = 64
KV_DIM = N_KV_HEADS * HEAD_DIM
ROPE_THETA = 10000.0
N_KEYS = 128
PEER_HEADS = 8
PEER_TOPK = 16
PEER_DQ = 256
EPS = 1e-6

LANES = 128
MIB = 1024 * 1024

IN_TN = 1024
COL_Z = 0
COL_XBC = 2
COL_Q = 6
COL_GA = 7
COL_GB = 8
COL_SMALL = 9
MAIN_W = 10 * IN_TN
SMALL_W = IN_TN
IN_MM_TN = 2048
IN_A_BLOCKS = (SSD_D_INNER + XBC_DIM) // IN_MM_TN

NT_DIMS = (((1,), (1,)), ((), ()))


def _cparams(sem, vmem_mib):
    return pltpu.CompilerParams(dimension_semantics=sem, vmem_limit_bytes=vmem_mib * MIB)


def _sigmoid(x):
    return 1.0 / (1.0 + jnp.exp(-x))


def _silu(x):
    return x * _sigmoid(x)


def _softplus(x):
    return jnp.maximum(x, 0.0) + jnp.log(1.0 + jnp.exp(-jnp.abs(x)))


def _ada_kernel(c_ref, w_ref, b_ref, o_ref):
    sc = _silu(c_ref[...])
    o_ref[...] = jnp.dot(sc.astype(BF16), w_ref[...].astype(BF16),
                         preferred_element_type=F32) + b_ref[...]


def _ada(cc, w, b):
    rows, d = cc.shape
    n = w.shape[1]
    tn = 1536
    return pl.pallas_call(
        _ada_kernel,
        grid=(n // tn,),
        in_specs=[pl.BlockSpec((rows, d), lambda j: (0, 0)),
                  pl.BlockSpec((d, tn), lambda j: (0, j)),
                  pl.BlockSpec((1, tn), lambda j: (0, j))],
        out_specs=pl.BlockSpec((rows, tn), lambda j: (0, j)),
        out_shape=jax.ShapeDtypeStruct((rows, n), F32),
        compiler_params=_cparams(("arbitrary",), 32),
        name="ada",
    )(cc, w, b)


def _inproj_kernel(x_ref, mod_ref, nw_ref, wa_ref, wb_ref, main_ref, small_ref, h_ref):
    j = pl.program_id(2)

    @pl.when(j == 0)
    def _():
        x = x_ref[...]
        r = lax.rsqrt(jnp.mean(x * x, axis=-1, keepdims=True) + EPS)
        shift = mod_ref[:, 0:D_MODEL]
        scale = mod_ref[:, D_MODEL:2 * D_MODEL]
        h_ref[...] = ((x * r * nw_ref[...]) * (1.0 + scale) + shift).astype(BF16)

    @pl.when(j < IN_A_BLOCKS)
    def _():
        main_ref[...] = jnp.dot(h_ref[...], wa_ref[...], preferred_element_type=F32).astype(BF16)

    @pl.when(j >= IN_A_BLOCKS)
    def _():
        acc = jnp.dot(h_ref[...], wb_ref[...], preferred_element_type=F32)
        main_ref[...] = acc.astype(BF16)

        @pl.when(j == MAIN_W // IN_MM_TN - 1)
        def _():
            lo = COL_SMALL * IN_TN - (MAIN_W - IN_MM_TN)
            small_ref[...] = acc[:, lo:lo + SMALL_W]


def _inproj(x, mod, nw, w_a, w_b, tm):
    bm, t, d = x.shape
    return pl.pallas_call(
        _inproj_kernel,
        grid=(bm, t // tm, MAIN_W // IN_MM_TN),
        in_specs=[pl.BlockSpec((None, tm, d), lambda b, i, j: (b, i, 0)),
                  pl.BlockSpec((None, 1, 6 * d), lambda b, i, j: (b, 0, 0)),
                  pl.BlockSpec((1, d), lambda b, i, j: (0, 0)),
                  pl.BlockSpec((d, IN_MM_TN), lambda b, i, j: (0, jnp.minimum(j, IN_A_BLOCKS - 1))),
                  pl.BlockSpec((d, IN_MM_TN), lambda b, i, j: (0, jnp.maximum(j - IN_A_BLOCKS, 0)))],
        out_specs=[pl.BlockSpec((None, tm, IN_MM_TN), lambda b, i, j: (b, i, j)),
                   pl.BlockSpec((None, tm, SMALL_W), lambda b, i, j: (b, i, 0))],
        out_shape=[jax.ShapeDtypeStruct((bm, t, MAIN_W), BF16),
                   jax.ShapeDtypeStruct((bm, t, SMALL_W), F32)],
        scratch_shapes=[pltpu.VMEM((tm, d), BF16)],
        compiler_params=_cparams(("arbitrary", "arbitrary", "arbitrary"), 40),
        name="inproj",
    )(x, mod, nw, w_a, w_b)


CONV_HALO = 16


CONV_SUB = 128
CONV_TC = 2048


def _conv_shift_matrices():
    s = np.zeros((CONV_W, CONV_SUB, CONV_SUB + CONV_HALO), np.float32)
    for k in range(CONV_W):
        for t in range(CONV_SUB):
            src = t + k - CONV_W // 2
            col = src if 0 <= src < CONV_SUB else CONV_SUB + 8 + (src if src < 0 else src - CONV_SUB)
            s[k, t, col] = 1.0
    return jnp.asarray(s, BF16)


def _conv_kernel(cur_ref, prev_ref, next_ref, s_ref, w_ref, b_ref, o_ref, *, nt, tt):
    i = pl.program_id(1)
    pv = jnp.where(i > 0, prev_ref[...].astype(F32)[8:16], 0.0)
    nx = jnp.where(i < nt - 1, next_ref[...].astype(F32)[0:8], 0.0)
    mid = CONV_W // 2
    for hb in range(tt // CONV_SUB):
        r0 = hb * CONV_SUB
        cur = cur_ref[r0:r0 + CONV_SUB, :]
        before = pv if hb == 0 else cur_ref[r0 - CONV_HALO:r0, :].astype(F32)[8:16]
        after = nx if r0 + CONV_SUB == tt else cur_ref[r0 + CONV_SUB:r0 + CONV_SUB + CONV_HALO, :].astype(F32)[0:8]
        ext = jnp.concatenate([cur, jnp.concatenate([before, after], axis=0).astype(BF16)], axis=0)
        acc = b_ref[...] + w_ref[mid:mid + 1, :] * cur.astype(F32)
        for k in range(CONV_W):
            if k != mid:
                acc = acc + w_ref[k:k + 1, :] * jnp.dot(s_ref[k], ext, preferred_element_type=F32)
        o_ref[r0:r0 + CONV_SUB, :] = _silu(acc).astype(BF16)


def _conv(main, conv_w, conv_b, tt):
    b, t, _ = main.shape
    tc = CONV_TC
    nt = t // tt
    hb = tt // CONV_HALO
    last = t // CONV_HALO - 1
    col0 = COL_XBC * IN_TN // tc
    shifts = _conv_shift_matrices()
    return pl.pallas_call(
        functools.partial(_conv_kernel, nt=nt, tt=tt),
        grid=(b, nt, XBC_DIM // tc),
        in_specs=[pl.BlockSpec((None, tt, tc), lambda bb, i, j: (bb, i, col0 + j)),
                  pl.BlockSpec((None, CONV_HALO, tc),
                               lambda bb, i, j: (bb, jnp.maximum(i * hb - 1, 0), col0 + j)),
                  pl.BlockSpec((None, CONV_HALO, tc),
                               lambda bb, i, j: (bb, jnp.minimum((i + 1) * hb, last), col0 + j)),
                  pl.BlockSpec(shifts.shape, lambda bb, i, j: (0, 0, 0)),
                  pl.BlockSpec((8, tc), lambda bb, i, j: (0, j)),
                  pl.BlockSpec((1, tc), lambda bb, i, j: (0, j))],
        out_specs=pl.BlockSpec((None, tt, tc), lambda bb, i, j: (bb, i, j)),
        out_shape=jax.ShapeDtypeStruct((b, t, XBC_DIM), BF16),
        compiler_params=_cparams(("arbitrary", "arbitrary", "arbitrary"), 32),
        name="conv",
    )(main, main, main, shifts, conv_w, conv_b)


N_PAIRS = SSD_HEADS // 2


def _ssd_kernel(*refs, nc, has_init):
    if has_init:
        xs_ref, b_ref, c_ref, dt_ref, prm_ref, init_ref, y_ref, fin_f_ref, fin_b_ref, st_ref = refs
    else:
        xs_ref, b_ref, c_ref, dt_ref, prm_ref, y_ref, fin_f_ref, fin_b_ref, st_ref = refs
    d = pl.program_id(0)
    c = pl.program_id(2)

    @pl.when(c == 0)
    def _():
        for pp in range(N_PAIRS):
            if has_init:
                st_ref[pp] = init_ref[pp * LANES:(pp + 1) * LANES, :].T
            else:
                st_ref[pp] = jnp.zeros((D_STATE, LANES), F32)

    row = lax.broadcasted_iota(jnp.int32, (CHUNK, CHUNK), 0)
    col = lax.broadcasted_iota(jnp.int32, (CHUNK, CHUNK), 1)
    tri = (row - col) * (1 - 2 * d) >= 0
    lo = col < SSD_HEADDIM

    a_log = prm_ref[0:1, :]
    bias = prm_ref[1:2, :]
    dt = _softplus(dt_ref[...] + bias)
    a = dt * (-jnp.exp(a_log))
    cs = jnp.dot(tri.astype(F32), a, precision=lax.Precision.HIGHEST,
                 preferred_element_type=F32)
    tot = jnp.sum(a, axis=0, keepdims=True)
    w_t = (jnp.exp(tot - cs) * dt).T
    r_t = (cs - jnp.log(dt)).T
    dec = jnp.exp(tot)

    for g in range(SSD_GROUPS):
        bg = b_ref[:, g * D_STATE:(g + 1) * D_STATE]
        cg = c_ref[:, g * D_STATE:(g + 1) * D_STATE]
        sc = lax.dot_general(cg, bg, NT_DIMS, preferred_element_type=F32)
        bg_t = bg.astype(F32).T
        cgf = cg.astype(F32)
        for q in range(2):
            pp = g * 2 + q
            xp = xs_ref[:, pp * LANES:(pp + 1) * LANES]
            st_old = st_ref[pp]
            rhs = jnp.concatenate([xp, st_old.astype(BF16)], axis=0)
            ys, us = [], []
            for e in range(2):
                h = pp * 2 + e
                csb = jnp.broadcast_to(cs[:, h:h + 1], (CHUNK, CHUNK))
                m = sc * jnp.exp(jnp.where(tri, csb - r_t[h:h + 1, :], -jnp.inf))
                cse = cgf * jnp.exp(csb)
                lhs = jnp.concatenate([m.astype(BF16), cse.astype(BF16)], axis=1)
                ys.append(jnp.dot(lhs, rhs, preferred_element_type=F32))
                bw = (bg_t * w_t[h:h + 1, :]).astype(BF16)
                us.append(jnp.dot(bw, xp, preferred_element_type=F32))
            y_ref[:, pp * LANES:(pp + 1) * LANES] = jnp.where(lo, ys[0], ys[1]).astype(BF16)
            decrow = jnp.where(lo[0:1, :], dec[:, 2 * pp:2 * pp + 1], dec[:, 2 * pp + 1:2 * pp + 2])
            st_ref[pp] = st_old * decrow + jnp.where(lo, us[0], us[1])

    for direction, fin_ref in ((0, fin_f_ref), (1, fin_b_ref)):
        @pl.when(jnp.logical_and(c == nc - 1, d == direction))
        def _(fin_ref=fin_ref):
            for pp in range(N_PAIRS):
                fin_ref[pp * LANES:(pp + 1) * LANES, :] = st_ref[pp].T


def _ssd(xbc, small, prm, init):
    b, t, _ = xbc.shape
    nc = t // CHUNK
    has_init = init is not None

    def ce(d, c):
        return c + d * (nc - 1 - 2 * c)

    in_specs = [pl.BlockSpec((None, CHUNK, SSD_D_INNER), lambda d, bb, c: (bb, ce(d, c), 0)),
                pl.BlockSpec((None, CHUNK, SSD_GROUPS * D_STATE), lambda d, bb, c: (bb, ce(d, c), 2)),
                pl.BlockSpec((None, CHUNK, SSD_GROUPS * D_STATE), lambda d, bb, c: (bb, ce(d, c), 3)),
                pl.BlockSpec((None, CHUNK, LANES), lambda d, bb, c: (bb, ce(d, c), 4 + d)),
                pl.BlockSpec((None, 8, LANES), lambda d, bb, c: (d, 0, 0))]
    args = [xbc, xbc, xbc, small, prm]
    if has_init:
        in_specs.append(pl.BlockSpec((None, None, SSD_D_INNER, D_STATE), lambda d, bb, c: (d, bb, 0, 0)))
        args.append(init)
    return pl.pallas_call(
        functools.partial(_ssd_kernel, nc=nc, has_init=has_init),
        grid=(2, b, nc),
        in_specs=in_specs,
        out_specs=[pl.BlockSpec((None, None, CHUNK, SSD_D_INNER), lambda d, bb, c: (d, bb, ce(d, c), 0)),
                   pl.BlockSpec((None, SSD_D_INNER, D_STATE), lambda d, bb, c: (jnp.where(d == 0, bb, b - 1), 0, 0)),
                   pl.BlockSpec((None, SSD_D_INNER, D_STATE), lambda d, bb, c: (jnp.where(d == 1, bb, 0), 0, 0))],
        out_shape=[jax.ShapeDtypeStruct((2, b, t, SSD_D_INNER), BF16),
                   jax.ShapeDtypeStruct((b, SSD_D_INNER, D_STATE), F32),
                   jax.ShapeDtypeStruct((b, SSD_D_INNER, D_STATE), F32)],
        scratch_shapes=[pltpu.VMEM((N_PAIRS, D_STATE, LANES), F32)],
        compiler_params=_cparams(("arbitrary", "arbitrary", "arbitrary"), 40),
        name="ssd",
    )(*args)


def _rope(x, cos, sin_a, sin_b, reps):
    n = x.shape[1]
    half = HEAD_DIM // 4
    cosf = jnp.concatenate([cos] * reps, axis=1)
    saf = jnp.concatenate([sin_a] * reps, axis=1)
    sbf = jnp.concatenate([sin_b] * reps, axis=1)
    return x * cosf + pltpu.roll(x, n - half, 1) * saf + pltpu.roll(x, half, 1) * sbf


def _prep_kernel(*refs, rope):
    if rope:
        q_ref, kv_ref, qnw_ref, knw_ref, bq_ref, bk_ref, cos_ref, sa_ref, sb_ref, qo_ref, ko_ref, vo_ref = refs
    else:
        q_ref, kv_ref, qnw_ref, knw_ref, bq_ref, bk_ref, qo_ref, ko_ref, vo_ref = refs
    q = q_ref[...].astype(F32)
    k = kv_ref[:, 0:KV_DIM]
    vo_ref[...] = kv_ref[:, KV_DIM:2 * KV_DIM]
    qss = jnp.dot((q * q).astype(BF16), bq_ref[...], preferred_element_type=F32)
    kss = jnp.dot((k * k).astype(BF16), bk_ref[...], preferred_element_type=F32)
    qn = q * lax.rsqrt(qss * (1.0 / HEAD_DIM) + EPS) * qnw_ref[...]
    kn = k * lax.rsqrt(kss * (1.0 / HEAD_DIM) + EPS) * knw_ref[...]
    if rope:
        cos, sa, sb = cos_ref[...], sa_ref[...], sb_ref[...]
        qn = _rope(qn, cos, sa, sb, D_MODEL // LANES)
        kn = _rope(kn, cos, sa, sb, KV_DIM // LANES)
    qo_ref[...] = (qn * (HEAD_DIM ** -0.5)).T.astype(BF16)
    ko_ref[...] = kn


def _prep(main, small, qnw_t, knw_t, bq, bk, tables, tt):
    b, t, _ = main.shape
    rope = tables is not None
    in_specs = [pl.BlockSpec((None, tt, D_MODEL), lambda bb, i: (bb, i, COL_Q)),
                pl.BlockSpec((None, tt, 2 * KV_DIM), lambda bb, i: (bb, i, 0)),
                pl.BlockSpec((1, D_MODEL), lambda bb, i: (0, 0)),
                pl.BlockSpec((1, KV_DIM), lambda bb, i: (0, 0)),
                pl.BlockSpec((D_MODEL, D_MODEL), lambda bb, i: (0, 0)),
                pl.BlockSpec((KV_DIM, KV_DIM), lambda bb, i: (0, 0))]
    args = [main, small, qnw_t, knw_t, bq, bk]
    if rope:
        in_specs += [pl.BlockSpec((tt, LANES), lambda bb, i: (i, 0))] * 3
        args += list(tables)
    return pl.pallas_call(
        functools.partial(_prep_kernel, rope=rope),
        grid=(b, t // tt),
        in_specs=in_specs,
        out_specs=[pl.BlockSpec((None, D_MODEL, tt), lambda bb, i: (bb, 0, i)),
                   pl.BlockSpec((None, tt, KV_DIM), lambda bb, i: (bb, i, 0)),
                   pl.BlockSpec((None, tt, KV_DIM), lambda bb, i: (bb, i, 0))],
        out_shape=[jax.ShapeDtypeStruct((b, D_MODEL, t), BF16),
                   jax.ShapeDtypeStruct((b, t, KV_DIM), F32),
                   jax.ShapeDtypeStruct((b, t, KV_DIM), F32)],
        compiler_params=_cparams(("arbitrary", "arbitrary"), 32),
        name="qk_prep",
    )(*args)


ATT_KC = 256
ATT_VROWS = 80


def _attn_kernel(*refs, has_ctx):
    if has_ctx:
        qt_ref, k_ref, v_ref, ck_ref, cv_ref, o_ref, ks_ref, vt_ref, ot_ref, sa_ref, sb_ref = refs
    else:
        qt_ref, k_ref, v_ref, o_ref, ks_ref, vt_ref, ot_ref, sa_ref, sb_ref = refs
    qi = pl.program_id(1)
    tq = qt_ref.shape[1]

    @pl.when(qi == 0)
    def _():
        parts = [(k_ref, v_ref, 0)]
        if has_ctx:
            parts.append((ck_ref, cv_ref, k_ref.shape[0]))
        for kr, vr, off in parts:
            n = kr.shape[0]
            lo = lax.broadcasted_iota(jnp.int32, (n, LANES), 1) < HEAD_DIM
            ones_pad = (lax.broadcasted_iota(jnp.int32, (ATT_VROWS - HEAD_DIM, n), 0) == 0).astype(F32)
            for pr in range(KV_DIM // LANES):
                a = kr[:, pr * LANES:(pr + 1) * LANES]
                ar = pltpu.roll(a, HEAD_DIM, 1)
                ks_ref[2 * pr, off:off + n, :] = jnp.where(lo, a, 0.0).astype(BF16)
                ks_ref[2 * pr + 1, off:off + n, :] = jnp.where(lo, ar, 0.0).astype(BF16)
                vt = vr[:, pr * LANES:(pr + 1) * LANES].T
                for e in range(2):
                    vt_ref[2 * pr + e, :, off:off + n] = jnp.concatenate(
                        [vt[e * HEAD_DIM:(e + 1) * HEAD_DIM], ones_pad], axis=0).astype(BF16)

    zq = jnp.zeros((LANES - HEAD_DIM, tq), BF16)
    hpg = N_HEADS // N_KV_HEADS
    tk = ks_ref.shape[1]
    chunks = [slice(c, min(c + ATT_KC, tk)) for c in range(0, tk, ATT_KC)]

    def scores(dst_ref, h):
        row = pl.multiple_of(h * HEAD_DIM, HEAD_DIM)
        qx = jnp.concatenate([qt_ref[pl.ds(row, HEAD_DIM), :], zq], axis=0)
        for c in chunks:
            dst_ref[c, :] = jnp.dot(ks_ref[h // hpg, c, :], qx, preferred_element_type=F32)

    def softmax_pv(s_ref, h):
        m = None
        for c in chunks:
            pm = jnp.max(s_ref[c, :], axis=0, keepdims=True)
            m = pm if m is None else jnp.maximum(m, pm)
        p = jnp.exp(s_ref[...] - m).astype(BF16)
        ox = jnp.dot(vt_ref[h // hpg], p, preferred_element_type=F32)
        row = pl.multiple_of(h * HEAD_DIM, HEAD_DIM)
        ot_ref[pl.ds(row, HEAD_DIM), :] = ox[0:HEAD_DIM] * (1.0 / ox[HEAD_DIM:HEAD_DIM + 1])

    scores(sa_ref, 0)

    def pair_body(hp, carry):
        h = 2 * hp
        scores(sb_ref, h + 1)
        softmax_pv(sa_ref, h)
        scores(sa_ref, jnp.minimum(h + 2, N_HEADS - 1))
        softmax_pv(sb_ref, h + 1)
        return carry

    lax.fori_loop(0, N_HEADS // 2, pair_body, 0)
    o_ref[...] = ot_ref[...].T.astype(BF16)


def _attn(qt, k, v, ctx, tq):
    b, _, t = qt.shape
    has_ctx = ctx is not None
    tk = k.shape[1] + (ctx[0].shape[1] if has_ctx else 0)
    kv_spec = lambda a: pl.BlockSpec((None, a.shape[1], KV_DIM), lambda bb, i: (bb, 0, 0))
    in_specs = [pl.BlockSpec((None, D_MODEL, tq), lambda bb, i: (bb, 0, i)), kv_spec(k), kv_spec(v)]
    args = [qt, k, v]
    if has_ctx:
        in_specs += [kv_spec(ctx[0]), kv_spec(ctx[1])]
        args += list(ctx)
    return pl.pallas_call(
        functools.partial(_attn_kernel, has_ctx=has_ctx),
        grid=(b, t // tq),
        in_specs=in_specs,
        out_specs=pl.BlockSpec((None, tq, D_MODEL), lambda bb, i: (bb, i, 0)),
        out_shape=jax.ShapeDtypeStruct((b, t, D_MODEL), BF16),
        scratch_shapes=[pltpu.VMEM((N_KV_HEADS, tk, LANES), BF16),
                        pltpu.VMEM((N_KV_HEADS, ATT_VROWS, tk), BF16),
                        pltpu.VMEM((D_MODEL, tq), F32),
                        pltpu.VMEM((tk, tq), F32), pltpu.VMEM((tk, tq), F32)],
        compiler_params=_cparams(("arbitrary", "arbitrary"), 48),
        name="attn",
    )(*args)


def _merge_kernel(yf_ref, yb_ref, xs_ref, z_ref, o_ref, ga_ref, gb_ref, x_ref, mod_ref,
                  dsk_ref, snw_ref, wso_ref, wao_ref, wo_ref, x1_ref):
    y = yf_ref[...].astype(F32) + yb_ref[...].astype(F32) + dsk_ref[...] * xs_ref[...].astype(F32)
    y = y * _silu(z_ref[...].astype(F32))
    y = y * lax.rsqrt(jnp.mean(y * y, axis=-1, keepdims=True) + EPS) * snw_ref[...]
    ya = jnp.dot(y.astype(BF16), wso_ref[...], preferred_element_type=F32)
    yb2 = jnp.dot(o_ref[...], wao_ref[...], preferred_element_type=F32)
    merged = _sigmoid(ga_ref[...].astype(F32)) * ya + _sigmoid(gb_ref[...].astype(F32)) * yb2
    mix = jnp.dot(merged.astype(BF16), wo_ref[...], preferred_element_type=F32)
    gate1 = mod_ref[:, 2 * D_MODEL:3 * D_MODEL]
    x1_ref[...] = x_ref[...] + gate1 * mix


def _merge(y2, xbc, main, o, x, mod, dsk, snw, wso, wao, wo, tm):
    bm, t, d = x.shape
    tok = lambda w, cb: pl.BlockSpec((None, tm, w), lambda b, i: (b, i, cb))
    full = lambda a: pl.BlockSpec(a.shape, lambda b, i: (0,) * a.ndim)
    return pl.pallas_call(
        _merge_kernel,
        grid=(bm, t // tm),
        in_specs=[pl.BlockSpec((None, None, tm, SSD_D_INNER), lambda b, i: (0, b, i, 0)),
                  pl.BlockSpec((None, None, tm, SSD_D_INNER), lambda b, i: (1, b, i, 0)),
                  tok(SSD_D_INNER, 0), tok(SSD_D_INNER, COL_Z), tok(d, 0),
                  tok(d, COL_GA), tok(d, COL_GB), tok(d, 0),
                  pl.BlockSpec((None, 1, 6 * d), lambda b, i: (b, 0, 0)),
                  full(dsk), full(snw), full(wso), full(wao), full(wo)],
        out_specs=tok(d, 0),
        out_shape=jax.ShapeDtypeStruct((bm, t, d), F32),
        compiler_params=_cparams(("arbitrary", "arbitrary"), 48),
        name="merge",
    )(y2, y2, xbc, main, o, main, main, x, mod, dsk, snw, wso, wao, wo)


def _top_rows(s, k, ranks=False):
    rows = []
    rank = jnp.full(s.shape, float(k), F32) if ranks else None
    for it in range(k):
        m = jnp.max(s, axis=0, keepdims=True)
        rows.append(m)
        hit = s == m
        s = jnp.where(hit, -jnp.inf, s)
        if ranks:
            rank = jnp.where(hit, float(it), rank)
    return (rows, rank) if ranks else rows


def _peer_front_kernel(x_ref, mod_ref, nw_ref, wpq_ref, keys_ref, ht_ref, a1_ref, n1_ref, r2_ref, e2_ref):
    x = x_ref[...]
    r = lax.rsqrt(jnp.mean(x * x, axis=-1, keepdims=True) + EPS)
    shift = mod_ref[:, 3 * D_MODEL:4 * D_MODEL]
    scale = mod_ref[:, 4 * D_MODEL:5 * D_MODEL]
    h = (x * r * nw_ref[...]) * (1.0 + scale) + shift
    hb = h.astype(BF16)
    ht_ref[...] = h.T.astype(BF16)
    qp = jnp.dot(hb, wpq_ref[...], preferred_element_type=F32).astype(BF16)
    k = PEER_TOPK
    for hd in range(PEER_HEADS):
        s1 = lax.dot_general(keys_ref[2 * hd], qp[:, (2 * hd) * LANES:(2 * hd + 1) * LANES],
                             NT_DIMS, preferred_element_type=F32)
        s2 = lax.dot_general(keys_ref[2 * hd + 1], qp[:, (2 * hd + 1) * LANES:(2 * hd + 2) * LANES],
                             NT_DIMS, preferred_element_type=F32)
        ta = _top_rows(s1, k)
        tb_rows, rank2 = _top_rows(s2, k, ranks=True)
        tb = jnp.concatenate(tb_rows, axis=0)
        pieces = [ta[0] + tb] + [ta[p] + tb[0:8] for p in range(1, 8)]
        pieces.append(jnp.concatenate(ta[8:], axis=0) + tb[0:1])
        best = _top_rows(jnp.concatenate(pieces, axis=0), k)
        zsum = jnp.ones_like(best[0])
        for v in best[1:]:
            zsum = zsum + jnp.exp(v - best[0])
        thr = best[-1]
        ta_arr = jnp.concatenate(ta, axis=0)
        n1 = jnp.zeros_like(s1)
        for q in range(k):
            theta = jnp.min(jnp.where(ta_arr + tb_rows[q] >= thr, ta_arr, jnp.inf), axis=0, keepdims=True)
            n1 = jnp.where(s1 >= theta, float(q + 1), n1)
        a1_ref[hd] = jnp.exp(s1 - ta[0])
        n1_ref[hd] = n1
        r2_ref[hd] = rank2.astype(BF16)
        e2_ref[hd] = (jnp.exp(s2 - tb_rows[0]) / zsum).astype(BF16)


def _peer_front(x1, mod, nw, wpq, keys, tm):
    bm, t, d = x1.shape
    sel = lambda: pl.BlockSpec((None, PEER_HEADS, N_KEYS, tm), lambda b, i: (b, 0, 0, i))
    sel_shape = lambda dt: jax.ShapeDtypeStruct((bm, PEER_HEADS, N_KEYS, t), dt)
    return pl.pallas_call(
        _peer_front_kernel,
        grid=(bm, t // tm),
        in_specs=[pl.BlockSpec((None, tm, d), lambda b, i: (b, i, 0)),
                  pl.BlockSpec((None, 1, 6 * d), lambda b, i: (b, 0, 0)),
                  pl.BlockSpec((1, d), lambda b, i: (0, 0)),
                  pl.BlockSpec(wpq.shape, lambda b, i: (0, 0)),
                  pl.BlockSpec(keys.shape, lambda b, i: (0, 0, 0))],
        out_specs=[pl.BlockSpec((None, d, tm), lambda b, i: (b, 0, i)), sel(), sel(), sel(), sel()],
        out_shape=[jax.ShapeDtypeStruct((bm, d, t), BF16), sel_shape(F32), sel_shape(F32),
                   sel_shape(BF16), sel_shape(BF16)],
        compiler_params=_cparams(("arbitrary", "arbitrary"), 48),
        name="peer_front",
    )(x1, mod, nw, wpq, keys)


PEER_EB = 2048
PEER_RB = 16
PEER_LW = 256
PEER_VK = 4


def _gelu_tanh(x):
    return 0.5 * x * (1.0 + jnp.tanh(0.7978845608028654 * (x + 0.044715 * (x * x * x))))


def _peer_dense_kernel(ht_ref, a1_ref, n1_ref, r2_ref, e2_ref, u_ref, v_ref, x1_ref, mod_ref, fw_ref,
                       y_ref, acc_ref, gt_ref, *, ne):
    e = pl.program_id(2)
    tt = ht_ref.shape[1]
    n_sub = PEER_EB // N_KEYS

    @pl.when(e == 0)
    def _():
        acc_ref[...] = jnp.zeros_like(acc_ref)

    n_rb = N_KEYS // PEER_RB

    def activations(sub):
        act = jnp.dot(u_ref[sub * N_KEYS:(sub + 1) * N_KEYS, :], ht_ref[...], preferred_element_type=F32)
        return _gelu_tanh(act.astype(BF16))

    def v_matmul(sub_hi):
        ch = slice((sub_hi + 1 - PEER_VK) * N_KEYS, (sub_hi + 1) * N_KEYS)
        acc_ref[...] += lax.dot_general(v_ref[ch, :], gt_ref[ch, :], (((0,), (0,)), ((), ())),
                                        preferred_element_type=F32)

    g_next = activations(0)
    for sub in range(n_sub):
        g = g_next
        if sub + 1 < n_sub:
            g_next = activations(sub + 1)
        if sub >= PEER_VK and sub % PEER_VK == 0:
            v_matmul(sub - 1)
        i = e * n_sub + sub
        a1_rows = [a1_ref[hd, pl.ds(i, 1), :] for hd in range(PEER_HEADS)]
        n1_rows = [n1_ref[hd, pl.ds(i, 1), :] for hd in range(PEER_HEADS)]
        for lg in range(tt // PEER_LW):
            ls = slice(lg * PEER_LW, (lg + 1) * PEER_LW)
            w = [None] * n_rb
            for hd in range(PEER_HEADS):
                a1 = jnp.broadcast_to(a1_rows[hd][:, ls], (PEER_RB, PEER_LW)).astype(BF16)
                n1 = jnp.broadcast_to(n1_rows[hd][:, ls], (PEER_RB, PEER_LW)).astype(BF16)
                for rb in range(n_rb):
                    rows = slice(rb * PEER_RB, (rb + 1) * PEER_RB)
                    wh = a1 * jnp.where(r2_ref[hd, rows, ls] < n1, e2_ref[hd, rows, ls], jnp.zeros((), BF16))
                    w[rb] = wh if w[rb] is None else w[rb] + wh
            for rb in range(n_rb):
                rows = slice(rb * PEER_RB, (rb + 1) * PEER_RB)
                gt_ref[sub * N_KEYS + rb * PEER_RB:sub * N_KEYS + (rb + 1) * PEER_RB, ls] = w[rb] * g[rows, ls]
    v_matmul(n_sub - 1)

    @pl.when(e == ne - 1)
    def _():
        gate2 = mod_ref[:, 5 * D_MODEL:6 * D_MODEL]
        x2 = x1_ref[...] + gate2 * acc_ref[...].T
        y_ref[...] = x2 * lax.rsqrt(jnp.mean(x2 * x2, axis=-1, keepdims=True) + EPS) * fw_ref[...]


def _peer_dense(ht, a1, n1, r2, e2, u, v, x1, mod, fw, tt):
    bm, t, d = x1.shape
    ne = u.shape[0] // PEER_EB
    sel = lambda: pl.BlockSpec((None, PEER_HEADS, N_KEYS, tt), lambda b, i, e: (b, 0, 0, i))
    return pl.pallas_call(
        functools.partial(_peer_dense_kernel, ne=ne),
        grid=(bm, t // tt, ne),
        in_specs=[pl.BlockSpec((None, d, tt), lambda b, i, e: (b, 0, i)),
                  sel(), sel(), sel(), sel(),
                  pl.BlockSpec((PEER_EB, d), lambda b, i, e: (e, 0)),
                  pl.BlockSpec((PEER_EB, d), lambda b, i, e: (e, 0)),
                  pl.BlockSpec((None, tt, d), lambda b, i, e: (b, i, 0)),
                  pl.BlockSpec((None, 1, 6 * d), lambda b, i, e: (b, 0, 0)),
                  pl.BlockSpec((1, d), lambda b, i, e: (0, 0))],
        out_specs=pl.BlockSpec((None, tt, d), lambda b, i, e: (b, i, 0)),
        out_shape=jax.ShapeDtypeStruct((bm, t, d), F32),
        scratch_shapes=[pltpu.VMEM((d, tt), F32), pltpu.VMEM((PEER_EB, tt), BF16)],
        compiler_params=_cparams(("arbitrary", "arbitrary", "arbitrary"), 56),
        name="peer_dense",
    )(ht, a1, n1, r2, e2, u, v, x1, mod, fw)


def _rope_tables(t):
    pos_t = np.arange(t)
    j = np.arange(HEAD_DIM)
    blk = j // (HEAD_DIM // 2)
    jj = j % (HEAD_DIM // 2)
    half = HEAD_DIM // 4
    inv = ROPE_THETA ** (-(jj % half).astype(np.float32) / half)
    pos = jnp.where(blk[None, :] == 0, (pos_t // GRID_W)[:, None], (pos_t % GRID_W)[:, None]).astype(F32)
    ang = pos * jnp.asarray(inv, F32)[None, :]
    cos, sin = jnp.cos(ang), jnp.sin(ang)
    first = jnp.asarray(jj < half)[None, :]
    sin_a = jnp.where(first, -sin, 0.0)
    sin_b = jnp.where(first, 0.0, sin)
    rep = lambda a: jnp.concatenate([a, a], axis=1)
    return rep(cos), rep(sin_a), rep(sin_b)


def _block_ones(n):
    idx = np.arange(n) // HEAD_DIM
    return jnp.asarray(idx[:, None] == idx[None, :], BF16)


def _group(x, mod, init, ctx_kv, seq_shape, wts, tables):
    bm, tmod, d = x.shape
    b, t = seq_shape
    seq = lambda a: a.reshape((b, t) + a.shape[2:])
    tok = lambda a: a.reshape((bm, tmod) + a.shape[2:])

    main, small = _inproj(x, mod, wts["norm1_w"], wts["w_in"], wts["w_tail"], 512)
    xbc = _conv(seq(main), wts["conv_w"], wts["conv_b"], 256)
    y2, fin_f, fin_b = _ssd(xbc, seq(small), wts["ssd_prm"], init)
    qn, kn, vn = _prep(seq(main), seq(small), wts["qnw_t"], wts["knw_t"], wts["bq"], wts["bk"], tables, 256)
    o = _attn(qn, kn, vn, ctx_kv, 256)
    y2m = y2.reshape((2, bm, tmod, SSD_D_INNER))
    x1 = _merge(y2m, tok(xbc), main, tok(o), x, mod, wts["dsk"], wts["ssd_norm_w"],
                wts["w_ssd_out"], wts["w_attn_out"], wts["w_o"], 256)
    ht, a1, n1, r2, e2 = _peer_front(x1, mod, wts["norm2_w"], wts["w_pq"], wts["peer_keys"], 256)
    y = _peer_dense(ht, a1, n1, r2, e2, wts["peer_u"], wts["peer_v"], x1, mod, wts["final_norm_w"], 512)
    return y, (fin_f, fin_b), kn, vn


def kernel(x_prompt, x_sample, cache_k, cache_v, state_ssm_fwd, state_ssm_bwd, c, c_ctx, w_ada, b_ada, norm1_w, norm2_w, w_in, conv_w, conv_b, a_log_f, a_log_b, dt_bias_f, dt_bias_b, d_skip, ssd_norm_w, w_ssd_out, q_norm_w, k_norm_w, w_attn_out, w_o, w_pq, peer_keys, peer_u, peer_v, final_norm_w):
    assert w_in.shape[0] == 1, "single-layer problem"
    bp, tp, d = x_prompt.shape
    bs, ts, _ = x_sample.shape

    wi = w_in[0].astype(BF16)
    offs = np.cumsum([0, SSD_D_INNER, XBC_DIM, SSD_HEADS, SSD_HEADS, D_MODEL, KV_DIM, KV_DIM, D_MODEL, D_MODEL])
    part = lambda n: wi[:, offs[n]:offs[n + 1]]
    zpad = lambda n: jnp.zeros((d, n), BF16)
    w_tail = jnp.concatenate(
        [part(4), part(7), part(8),
         part(5), part(6),
         part(2), zpad(LANES - SSD_HEADS), part(3), zpad(LANES - SSD_HEADS),
         zpad(SMALL_W - 2 * KV_DIM - 2 * LANES)], axis=1)
    row = lambda a: a.reshape(1, -1).astype(F32)
    padl = lambda a: jnp.pad(a.astype(F32), (0, LANES - a.shape[0]))
    prm = jnp.zeros((2, 8, LANES), F32)
    prm = prm.at[0, 0].set(padl(a_log_f[0])).at[0, 1].set(padl(dt_bias_f[0]))
    prm = prm.at[1, 0].set(padl(a_log_b[0])).at[1, 1].set(padl(dt_bias_b[0]))
    wts = {
        "norm1_w": row(norm1_w[0]), "norm2_w": row(norm2_w[0]), "w_in": wi, "w_tail": w_tail,
        "conv_w": jnp.pad(conv_w[0], ((0, 8 - CONV_W), (0, 0))), "conv_b": row(conv_b[0]),
        "ssd_prm": prm,
        "dsk": row(jnp.repeat(d_skip[0], SSD_HEADDIM)), "ssd_norm_w": row(ssd_norm_w[0]),
        "w_ssd_out": w_ssd_out[0].astype(BF16), "w_attn_out": w_attn_out[0].astype(BF16),
        "w_o": w_o[0].astype(BF16),
        "qnw_t": row(jnp.tile(q_norm_w[0], N_HEADS)), "knw_t": row(jnp.tile(k_norm_w[0], N_KV_HEADS)),
        "bq": _block_ones(D_MODEL), "bk": _block_ones(KV_DIM),
        "w_pq": w_pq[0].astype(BF16),
        "peer_keys": peer_keys[0].reshape(PEER_HEADS * 2, N_KEYS, PEER_DQ // 2).astype(BF16),
        "peer_u": peer_u[0].astype(BF16), "peer_v": peer_v[0].astype(BF16),
        "final_norm_w": row(final_norm_w),
    }

    cc = jnp.concatenate([c_ctx[None, :], c, jnp.zeros((8 - 1 - bs, d), F32)], axis=0)
    mod = _ada(cc, w_ada[0], row(b_ada[0]))
    mod_p = mod[0:1].reshape(1, 1, 6 * d)
    mod_s = mod[1:1 + bs].reshape(bs, 1, 6 * d)

    yp, fin_p, kn_p, vn_p = _group(x_prompt.reshape(1, bp * tp, d), mod_p, None, None, (bp, tp), wts, None)

    init = jnp.stack([state_ssm_fwd[:, 0], state_ssm_bwd[:, 0]], axis=0).reshape(2, bs, SSD_D_INNER, D_STATE)
    ctx_kv = (cache_k[:, 0].reshape(bs, -1, KV_DIM), cache_v[:, 0].reshape(bs, -1, KV_DIM))
    ys, _, _, _ = _group(x_sample, mod_s, init, ctx_kv, (bs, ts), wts, _rope_tables(ts))

    y_prompt = yp.reshape(bp, tp, d)
    new_cache_k = kn_p.reshape(bp, 1, tp, N_KV_HEADS, HEAD_DIM)
    new_cache_v = vn_p.reshape(bp, 1, tp, N_KV_HEADS, HEAD_DIM)
    new_state_fwd = fin_p[0].reshape(bp, 1, SSD_HEADS, SSD_HEADDIM, D_STATE)
    new_state_bwd = fin_p[1].reshape(bp, 1, SSD_HEADS, SSD_HEADDIM, D_STATE)
    return (y_prompt, ys, new_cache_k, new_cache_v, new_state_fwd, new_state_bwd)
```
